```python
import math
import jax, jax.numpy as jnp
from jax import lax
import numpy as np


D_MODEL = 1024
BATCH = 1
SEQ = 16384
DEPTH = 2
DEC_BATCH = 8
DEC_SEQ = 4096
PAST_LEN = 128

GRID_W = 64
GROUP_DIM = 16
N_GROUPS = D_MODEL // GROUP_DIM
STATE_DIM = 64
N_DIR = 2
N_HEADS = 16
HEAD_DIM = D_MODEL // N_HEADS
WIN_H = 8
WIN_W = 16
D_FF = ((8 * D_MODEL // 3 + 127) // 128) * 128
CONV_W = 3
N_MIXERS = 2
N_S5 = (DEPTH + 1) // 2
N_NA = DEPTH // 2
ALPHA = (2 * DEPTH) ** 0.25
BETA = (8 * DEPTH) ** -0.25
LN_EPS = 1e-5
DT_MIN = 1e-3
DT_MAX = 1e-1

kernel_name = 'hybrid_s5_natten_deepnorm_encoder'


def layer_norm(x, g, b):
    xf = x.astype(jnp.float32)
    mu = jnp.mean(xf, axis=-1, keepdims=True)
    var = jnp.mean(jnp.square(xf - mu), axis=-1, keepdims=True)
    y = (xf - mu) * lax.rsqrt(var + LN_EPS)
    return (y * g.astype(jnp.float32) + b.astype(jnp.float32)).astype(x.dtype)


def _cmul(ar, ai, br, bi):
    return ar * br - ai * bi, ar * bi + ai * br


def _ssm_combine(e1, e2):
    a1r, a1i, b1r, b1i = e1
    a2r, a2i, b2r, b2i = e2
    ar, ai = _cmul(a2r, a2i, a1r, a1i)
    br, bi = _cmul(a2r, a2i, b1r, b1i)
    return ar, ai, br + b2r, bi + b2i


def s5_direction(u, lam_re, lam_im, log_dt, b_re, b_im, c_re, c_im, reverse):
    dt = jnp.exp(log_dt)[:, None]
    z_re, z_im = lam_re * dt, lam_im * dt
    mag = jnp.exp(z_re)
    ab_re, ab_im = mag * jnp.cos(z_im), mag * jnp.sin(z_im)
    den = lam_re * lam_re + lam_im * lam_im
    nr, ni = ab_re - 1.0, ab_im
    f_re = (nr * lam_re + ni * lam_im) / den
    f_im = (ni * lam_re - nr * lam_im) / den
    bb_re = f_re[..., None] * b_re - f_im[..., None] * b_im
    bb_im = f_re[..., None] * b_im + f_im[..., None] * b_re
    bu_re = jnp.einsum('bsgh,gph->bsgp', u, bb_re)
    bu_im = jnp.einsum('bsgh,gph->bsgp', u, bb_im)
    a_re = jnp.broadcast_to(ab_re, bu_re.shape)
    a_im = jnp.broadcast_to(ab_im, bu_re.shape)
    _, _, s_re, s_im = lax.associative_scan(_ssm_combine, (a_re, a_im, bu_re, bu_im), axis=1, reverse=reverse)
    return jnp.einsum('bsgp,ghp->bsgh', s_re, c_re) - jnp.einsum('bsgp,ghp->bsgh', s_im, c_im)


def s5_mixer(h, lam_re, lam_im, log_dt, b_re, b_im, c_re, c_im, d_skip, w_glu, b_glu):
    bsz, slen, _ = h.shape
    f32 = jnp.float32
    u = h.astype(f32).reshape(bsz, slen, N_GROUPS, GROUP_DIM)
    y = d_skip.astype(f32) * u
    for d in range(N_DIR):
        y = y + s5_direction(u, lam_re[d].astype(f32), lam_im[d].astype(f32), log_dt[d].astype(f32),
                             b_re[d].astype(f32), b_im[d].astype(f32), c_re[d].astype(f32), c_im[d].astype(f32),
                             reverse=(d == 1))
    y = jax.nn.gelu(y.reshape(bsz, slen, D_MODEL)).astype(h.dtype)
    ag = y @ w_glu + b_glu
    a, g = jnp.split(ag, 2, axis=-1)
    return a * jax.nn.sigmoid(g)


def neighborhood_attention(h, w_qkv, b_qkv, rpb, w_o, b_o):
    bsz, slen, _ = h.shape
    rows = slen // GRID_W
    kh = min(WIN_H, rows)
    qkv = h @ w_qkv + b_qkv
    q, k, v = jnp.split(qkv, 3, axis=-1)
    grid = (bsz, rows, GRID_W, N_HEADS, HEAD_DIM)
    q = q.reshape(grid) * (HEAD_DIM ** -0.5)
    k = k.reshape(grid)
    v = v.reshape(grid)
    col = jnp.arange(GRID_W)
    col_start = jnp.clip(col - WIN_W // 2, 0, GRID_W - WIN_W)
    col_idx = col_start[:, None] + jnp.arange(WIN_W)[None, :]
    col_off = col_idx - col[:, None] + (WIN_W - 1)

    def one_row(r):
        rs = jnp.clip(r - kh // 2, 0, rows - kh)
        k_rows = lax.dynamic_slice_in_dim(k, rs, kh, axis=1)
        v_rows = lax.dynamic_slice_in_dim(v, rs, kh, axis=1)
        k_win = k_rows[:, :, col_idx]
        v_win = v_rows[:, :, col_idx]
        q_r = lax.dynamic_index_in_dim(q, r, axis=1, keepdims=False)
        s = jnp.einsum('bqhd,bkqwhd->bhqkw', q_r, k_win).astype(jnp.float32)
        row_off = rs + jnp.arange(kh) - r + (WIN_H - 1)
        bias = rpb.astype(jnp.float32)[:, row_off][:, :, col_off]
        s = s + jnp.transpose(bias, (0, 2, 1, 3))[None]
        p = jax.nn.softmax(s.reshape(bsz, N_HEADS, GRID_W, kh * WIN_W), axis=-1)
        p = p.reshape(bsz, N_HEADS, GRID_W, kh, WIN_W).astype(v.dtype)
        return jnp.einsum('bhqkw,bkqwhd->bqhd', p, v_win)

    o = lax.map(one_row, jnp.arange(rows))
    o = jnp.moveaxis(o, 0, 1).reshape(bsz, slen, D_MODEL)
    return o @ w_o + b_o


def conv_ffn(h, w_in, b_in, conv_w, conv_b, w_out, b_out):
    ug = h @ w_in + b_in
    u, g = jnp.split(ug, 2, axis=-1)
    up = jnp.pad(u, ((0, 0), (1, 1), (0, 0)))
    u = up[:, :-2] * conv_w[0] + up[:, 1:-1] * conv_w[1] + up[:, 2:] * conv_w[2] + conv_b
    return (jax.nn.gelu(u) * g) @ w_out + b_out


def encoder_trunk(x, c, w_ada, b_ada, ln_g, ln_b, s5_p, na_p, ffn_p):
    cond = jax.nn.silu(c)
    for i in range(DEPTH):
        mod = (cond @ w_ada[i] + b_ada[i])[:, None, :]
        sh_m, sc_m, g_m, sh_f, sc_f, g_f = jnp.split(mod, 6, axis=-1)
        h = x * (1.0 + sc_m) + sh_m
        j = i // N_MIXERS
        if i % N_MIXERS == 0:
            y = s5_mixer(h, *[p[j] for p in s5_p])
        else:
            y = neighborhood_attention(h, *[p[j] for p in na_p])
        x = layer_norm(ALPHA * x + g_m * y, ln_g[i, 0], ln_b[i, 0])
        h = x * (1.0 + sc_f) + sh_f
        y = conv_ffn(h, *[p[i] for p in ffn_p])
        x = layer_norm(ALPHA * x + g_f * y, ln_g[i, 1], ln_b[i, 1])
    return x


def setup_inputs(seed: int = 0) -> dict:
    key = jax.random.key(seed)
    ks = jax.random.split(key, 32)
    f32 = jnp.float32

    def nrm(k, shape, s):
        return s * jax.random.normal(k, shape, f32)

    D = D_MODEL
    n_idx = jnp.arange(STATE_DIM, dtype=f32)
    s5_shape = (N_S5, N_DIR, N_GROUPS, STATE_DIM)
    qkv_scale = jnp.concatenate([jnp.ones((2 * D,), f32), jnp.full((D,), BETA, f32)])
    return {
        'x_prompt': nrm(ks[0], (BATCH, SEQ, D), 1.0),
        'x_sample': nrm(ks[1], (DEC_BATCH, DEC_SEQ, D), 1.0),
        'c_prompt': nrm(ks[2], (BATCH, D), 1.0),
        'c_sample': nrm(ks[3], (DEC_BATCH, D), 1.0),
        'w_ada': nrm(ks[4], (DEPTH, D, 6 * D), 0.5 * D ** -0.5),
        'b_ada': nrm(ks[5], (DEPTH, 6 * D), 0.02),
        'ln_g': 1.0 + nrm(ks[6], (DEPTH, 2, D), 0.02),
        'ln_b': nrm(ks[7], (DEPTH, 2, D), 0.02),
        's5_lam_re': -0.5 + nrm(ks[8], s5_shape, 0.01),
        's5_lam_im': math.pi * n_idx + nrm(ks[9], s5_shape, 0.01),
        's5_log_dt': jax.random.uniform(ks[10], (N_S5, N_DIR, N_GROUPS), f32, math.log(DT_MIN), math.log(DT_MAX)),
        's5_b_re': nrm(ks[11], (N_S5, N_DIR, N_GROUPS, STATE_DIM, GROUP_DIM), (2 * GROUP_DIM) ** -0.5),
        's5_b_im': nrm(ks[12], (N_S5, N_DIR, N_GROUPS, STATE_DIM, GROUP_DIM), (2 * GROUP_DIM) ** -0.5),
        's5_c_re': nrm(ks[13], (N_S5, N_DIR, N_GROUPS, GROUP_DIM, STATE_DIM), (2 * STATE_DIM) ** -0.5),
        's5_c_im': nrm(ks[14], (N_S5, N_DIR, N_GROUPS, GROUP_DIM, STATE_DIM), (2 * STATE_DIM) ** -0.5),
        's5_d': nrm(ks[15], (N_S5, N_GROUPS, GROUP_DIM), 1.0),
        's5_w_glu': nrm(ks[16], (N_S5, D, 2 * D), BETA * D ** -0.5),
        's5_b_glu': nrm(ks[17], (N_S5, 2 * D), 0.02),
        'na_w_qkv': nrm(ks[18], (N_NA, D, 3 * D), D ** -0.5) * qkv_scale,
        'na_b_qkv': nrm(ks[19], (N_NA, 3 * D), 0.02),
        'na_rpb': nrm(ks[20], (N_NA, N_HEADS, 2 * WIN_H - 1, 2 * WIN_W - 1), 0.1),
        'na_w_o': nrm(ks[21], (N_NA, D, D), BETA * D ** -0.5),
        'na_b_o': nrm(ks[22], (N_NA, D), 0.02),
        'ffn_w_in': nrm(ks[23], (DEPTH, D, 2 * D_FF), D ** -0.5),
        'ffn_b_in': nrm(ks[24], (DEPTH, 2 * D_FF), 0.02),
        'ffn_conv_w': nrm(ks[25], (DEPTH, CONV_W, D_FF), CONV_W ** -0.5),
        'ffn_conv_b': nrm(ks[26], (DEPTH, D_FF), 0.02),
        'ffn_w_out': nrm(ks[27], (DEPTH, D_FF, D), BETA * D_FF ** -0.5),
        'ffn_b_out': nrm(ks[28], (DEPTH, D), 0.02),
    }


def reference(x_prompt, x_sample, c_prompt, c_sample, w_ada, b_ada, ln_g, ln_b,
              s5_lam_re, s5_lam_im, s5_log_dt, s5_b_re, s5_b_im, s5_c_re, s5_c_im, s5_d, s5_w_glu, s5_b_glu,
              na_w_qkv, na_b_qkv, na_rpb, na_w_o, na_b_o,
              ffn_w_in, ffn_b_in, ffn_conv_w, ffn_conv_b, ffn_w_out, ffn_b_out):
    s5_p = (s5_lam_re, s5_lam_im, s5_log_dt, s5_b_re, s5_b_im, s5_c_re, s5_c_im, s5_d, s5_w_glu, s5_b_glu)
    na_p = (na_w_qkv, na_b_qkv, na_rpb, na_w_o, na_b_o)
    ffn_p = (ffn_w_in, ffn_b_in, ffn_conv_w, ffn_conv_b, ffn_w_out, ffn_b_out)
    y_prompt = encoder_trunk(x_prompt, c_prompt, w_ada, b_ada, ln_g, ln_b, s5_p, na_p, ffn_p)
    y_sample = encoder_trunk(x_sample, c_sample, w_ada, b_ada, ln_g, ln_b, s5_p, na_p, ffn_p)
    return (y_prompt, y_sample)
```

```python
import functools
import math

import numpy as np
import jax
import jax.numpy as jnp
from jax import lax
from jax.experimental import pallas as pl
from jax.experimental.pallas import tpu as pltpu

_D = 1024
_DEPTH = 2
_GRID_W = 64
_GROUP_DIM = 16
_N_GROUPS = _D // _GROUP_DIM
_STATE_DIM = 64
_N_HEADS = 16
_HEAD_DIM = _D // _N_HEADS
_WIN_H = 8
_WIN_W = 16
_D_FF = 2816
_ALPHA = (2 * _DEPTH) ** 0.25
_LN_EPS = 1e-5

_LANES = 128
_SUBLANES = 8
_BF16_ROWS = 16
_VMEM_LIMIT = 56 * 1024 * 1024

_L = 16
_GW = _L * _GROUP_DIM
_PW = 2 * _GW
_N_PAIRS = _N_GROUPS // 2
_SW = 2 * _STATE_DIM
_CHUNK_W = _N_GROUPS * _GW
_STATE_W = _N_GROUPS * _STATE_DIM

_TM = 512
_FFN_SPLIT = (0, 1536, _D_FF)
_Q_ROWS = 8
_BAND = _WIN_H * _GRID_W
_NEG = -1e30

_f32 = jnp.float32
_bf16 = jnp.bfloat16


def _params(sem):
    return pltpu.CompilerParams(dimension_semantics=sem, vmem_limit_bytes=_VMEM_LIMIT)


def _const_spec(shape):
    nd = len(shape)
    return pl.BlockSpec(shape, lambda *_: (0,) * nd, pipeline_mode=pl.Buffered(1))


def _layer_norm(z, g, b):
    mu = jnp.mean(z, axis=-1, keepdims=True)
    d = z - mu
    var = jnp.mean(d * d, axis=-1, keepdims=True)
    return d * lax.rsqrt(var + _LN_EPS) * g + b


def _dot(a, b):
    return jnp.dot(a, b, preferred_element_type=_f32)


def _mod_kernel(c_ref, w_ref, b_ref, o_ref):
    cond = jax.nn.silu(c_ref[...]).astype(_bf16)
    o_ref[...] = _dot(cond, w_ref[...].astype(_bf16)) + b_ref[...]


def _modulation(c_all, w_ada, b_ada):
    nb = c_all.shape[0]
    tn = 1536
    return pl.pallas_call(
        _mod_kernel,
        grid=(_DEPTH, 6 * _D // tn),
        in_specs=[
            pl.BlockSpec((nb, _D), lambda l, n: (0, 0)),
            pl.BlockSpec((None, _D, tn), lambda l, n: (l, 0, n)),
            pl.BlockSpec((None, 1, tn), lambda l, n: (l, 0, n)),
        ],
        out_specs=pl.BlockSpec((None, nb, tn), lambda l, n: (l, 0, n)),
        out_shape=jax.ShapeDtypeStruct((_DEPTH, nb, 6 * _D), _f32),
        compiler_params=_params(("arbitrary", "arbitrary")),
        name="adaln_mod",
    )(c_all, w_ada, b_ada.reshape(_DEPTH, 1, 6 * _D))


def _s5_tables(lam_re, lam_im, log_dt, b_re, b_im, c_re, c_im):
    hi = lax.Precision.HIGHEST
    dt = jnp.exp(log_dt)[..., None]
    z_re, z_im = lam_re * dt, lam_im * dt
    mag = jnp.exp(z_re)
    ab_re, ab_im = mag * jnp.cos(z_im), mag * jnp.sin(z_im)
    den = lam_re * lam_re + lam_im * lam_im
    nr, ni = ab_re - 1.0, ab_im
    f_re = (nr * lam_re + ni * lam_im) / den
    f_im = (ni * lam_re - nr * lam_im) / den
    bb_re = f_re[..., None] * b_re - f_im[..., None] * b_im
    bb_im = f_re[..., None] * b_im + f_im[..., None] * b_re

    k = jnp.arange(_L + 1, dtype=_f32)[:, None, None, None]
    pw_mag = jnp.exp(k * z_re)
    pw_re, pw_im = pw_mag * jnp.cos(k * z_im), pw_mag * jnp.sin(k * z_im)

    abb_re = pw_re[..., None] * bb_re - pw_im[..., None] * bb_im
    abb_im = pw_re[..., None] * bb_im + pw_im[..., None] * bb_re

    lag = (jnp.einsum('kdgph,dgop->kdgho', abb_re[:_L], c_re, precision=hi)
           - jnp.einsum('kdgph,dgop->kdgho', abb_im[:_L], c_im, precision=hi))
    s_idx = np.arange(_L)[:, None]
    t_idx = np.arange(_L)[None, :]
    kf = lag[np.clip(t_idx - s_idx, 0, _L - 1), 0]
    kb = lag[np.clip(s_idx - t_idx, 0, _L - 1), 1]
    mf = jnp.asarray(t_idx >= s_idx, _f32)[:, :, None, None, None]
    mb = jnp.asarray(s_idx >= t_idx, _f32)[:, :, None, None, None]
    toep = (kf * mf + kb * mb).transpose(2, 0, 3, 1, 4).reshape(_N_GROUPS, _GW, _GW)

    rev = np.arange(_L - 1, -1, -1)
    st = jnp.stack([abb_re[rev, 0], abb_im[rev, 0], abb_re[:_L, 1], abb_im[:_L, 1]], axis=0)
    st = st.transpose(2, 1, 4, 0, 3).reshape(_N_PAIRS, 2, _GW, 4, _STATE_DIM)
    eye2 = jnp.eye(2, dtype=_f32)
    w_state = jnp.einsum('rexqp,ef->rexqfp', st, eye2).reshape(_N_PAIRS, _PW, 4 * _SW)

    def out_mat(d, powers):
        pr_, pi_ = pw_re[powers, d], pw_im[powers, d]
        cr, ci = c_re[d], c_im[d]
        wr = cr[None] * pr_[:, :, None, :] - ci[None] * pi_[:, :, None, :]
        wi = cr[None] * pi_[:, :, None, :] + ci[None] * pr_[:, :, None, :]
        return wr, -wi

    wfr, wfi = out_mat(0, np.arange(1, _L + 1))
    wbr, wbi = out_mat(1, np.arange(_L, 0, -1))
    wo = jnp.stack([wfr, wfi, wbr, wbi], axis=0)
    wo = wo.transpose(2, 0, 4, 1, 3).reshape(_N_PAIRS, 2, 4, _STATE_DIM, _GW)
    w_out = jnp.einsum('reqpy,ef->rqepfy', wo, eye2).reshape(_N_PAIRS, 4 * _SW, _PW)

    decay = jnp.stack([pw_re[_L, 0], pw_im[_L, 0], pw_re[_L, 1], pw_im[_L, 1]], axis=0).reshape(4, _STATE_W)
    return toep.astype(_bf16), w_state.astype(_bf16), w_out.astype(_bf16), decay


def _s5_state_kernel(x_ref, sc_ref, sh_ref, w_ref, ofr, ofi, obr, obi):
    nb, cs, _ = x_ref.shape
    u = x_ref[...] * (1.0 + sc_ref[...]) + sh_ref[...]
    r = _dot(u.reshape(nb * cs, _PW).astype(_bf16), w_ref[...]).reshape(nb, cs, 4 * _SW)
    for q, o_ref in enumerate((ofr, ofi, obr, obi)):
        o_ref[...] = r[:, :, q * _SW:(q + 1) * _SW]


def _s5_scan_kernel(lfr, lfi, lbr, lbi, dec_ref, ofr, ofi, obr, obi, carry):
    @pl.when(pl.program_id(1) == 0)
    def _():
        carry[...] = jnp.zeros_like(carry)

    n = lfr.shape[0]
    afr, afi, abr, abi = dec_ref[0], dec_ref[1], dec_ref[2], dec_ref[3]

    def body(i, c):
        fr, fi, br, bi = c
        ib = n - 1 - i
        ofr[i] = fr
        ofi[i] = fi
        obr[ib] = br
        obi[ib] = bi
        return (afr * fr - afi * fi + lfr[i], afr * fi + afi * fr + lfi[i],
                abr * br - abi * bi + lbr[ib], abr * bi + abi * br + lbi[ib])

    c = lax.fori_loop(0, n, body, (carry[0], carry[1], carry[2], carry[3]), unroll=4)
    for q in range(4):
        carry[q] = c[q]


def _s5_out_kernel(x_ref, sc_ref, sh_ref, d_ref, sfr, sfi, sbr, sbi, t_ref, w_ref, o_ref):
    nb, cs, _ = x_ref.shape
    m = nb * cs
    u = (x_ref[...] * (1.0 + sc_ref[...]) + sh_ref[...]).reshape(m, _PW)
    ub = u.astype(_bf16)
    st = jnp.concatenate([sfr[...], sfi[...], sbr[...], sbi[...]], axis=-1).reshape(m, 4 * _SW)
    y_state = _dot(st.astype(_bf16), w_ref[...])
    y_local = jnp.concatenate([_dot(ub[:, :_GW], t_ref[0]), _dot(ub[:, _GW:], t_ref[1])], axis=-1)
    y = d_ref[...] * u + y_state + y_local
    o_ref[...] = jax.nn.gelu(y).astype(_bf16).reshape(nb, cs, _PW)


def _glu_ln_kernel(y_ref, x_ref, w_ref, b_ref, gate_ref, lng_ref, lnb_ref, o_ref):
    ag = _dot(y_ref[...], w_ref[...]) + b_ref[...]
    mix = ag[:, :_D] * jax.nn.sigmoid(ag[:, _D:])
    z = _ALPHA * x_ref[...] + gate_ref[...] * mix
    o_ref[...] = _layer_norm(z, lng_ref[...], lnb_ref[...])


def _chunk_tile(v):
    b = v.shape[0]
    t = jnp.broadcast_to(v.reshape(b, _N_GROUPS, 1, _GROUP_DIM), (b, _N_GROUPS, _L, _GROUP_DIM))
    return t.reshape(b, 1, _CHUNK_W)


def _s5_layer(x, sc, sh, gate, tables, d_skip, w_glu, b_glu, ln_g, ln_b):
    bsz, slen, _ = x.shape
    cs = slen // _L
    toep, w_state, w_out, decay = tables
    xt = x.reshape(bsz, cs, _L, _N_GROUPS, _GROUP_DIM).transpose(0, 1, 3, 2, 4).reshape(bsz, cs, _CHUNK_W)
    sc_t, sh_t = _chunk_tile(sc), _chunk_tile(sh)
    d_t = _chunk_tile(d_skip.reshape(1, _D))[0]

    x_spec = pl.BlockSpec((bsz, cs, _PW), lambda p: (0, 0, p))
    mod_spec = pl.BlockSpec((bsz, 1, _PW), lambda p: (0, 0, p))
    st_spec = pl.BlockSpec((bsz, cs, _SW), lambda p: (0, 0, p))
    st_shape = jax.ShapeDtypeStruct((bsz, cs, _STATE_W), _f32)

    loc = pl.pallas_call(
        _s5_state_kernel,
        grid=(_N_PAIRS,),
        in_specs=[x_spec, mod_spec, mod_spec,
                  pl.BlockSpec((None, _PW, 4 * _SW), lambda p: (p, 0, 0))],
        out_specs=[st_spec] * 4,
        out_shape=[st_shape] * 4,
        compiler_params=_params(("arbitrary",)),
        name="s5_chunk_states",
    )(xt, sc_t, sh_t, w_state)

    sub = _STATE_W // _SUBLANES
    cblk = min(cs, 64)
    nblk = cs // cblk
    loc4 = [a.reshape(bsz, cs, _SUBLANES, sub) for a in loc]
    fwd_spec = pl.BlockSpec((None, cblk, _SUBLANES, sub), lambda b, j: (b, j, 0, 0))
    bwd_spec = pl.BlockSpec((None, cblk, _SUBLANES, sub), lambda b, j: (b, nblk - 1 - j, 0, 0))
    scan_shape = jax.ShapeDtypeStruct((bsz, cs, _SUBLANES, sub), _f32)
    sin = pl.pallas_call(
        _s5_scan_kernel,
        grid=(bsz, nblk),
        in_specs=[fwd_spec, fwd_spec, bwd_spec, bwd_spec,
                  pl.BlockSpec((4, _SUBLANES, sub), lambda b, j: (0, 0, 0))],
        out_specs=[fwd_spec, fwd_spec, bwd_spec, bwd_spec],
        out_shape=[scan_shape] * 4,
        scratch_shapes=[pltpu.VMEM((4, _SUBLANES, sub), _f32)],
        compiler_params=_params(("arbitrary", "arbitrary")),
        name="s5_chunk_scan",
    )(*loc4, decay.reshape(4, _SUBLANES, sub))
    sin = [a.reshape(bsz, cs, _STATE_W) for a in sin]

    yt = pl.pallas_call(
        _s5_out_kernel,
        grid=(_N_PAIRS,),
        in_specs=[x_spec, mod_spec, mod_spec,
                  pl.BlockSpec((1, _PW), lambda p: (0, p)),
                  st_spec, st_spec, st_spec, st_spec,
                  pl.BlockSpec((2, _GW, _GW), lambda p: (p, 0, 0)),
                  pl.BlockSpec((None, 4 * _SW, _PW), lambda p: (p, 0, 0))],
        out_specs=x_spec,
        out_shape=jax.ShapeDtypeStruct((bsz, cs, _CHUNK_W), _bf16),
        compiler_params=_params(("arbitrary",)),
        name="s5_chunk_out",
    )(xt, sc_t, sh_t, d_t, *sin, toep, w_out)

    y = yt.reshape(bsz, cs, _N_GROUPS, _L, _GROUP_DIM).transpose(0, 1, 3, 2, 4).reshape(bsz * slen, _D)

    nt = bsz * slen
    bps = slen // _TM
    row_spec = pl.BlockSpec((_TM, _D), lambda i: (i, 0))
    seq_spec = pl.BlockSpec((None, 1, _D), lambda i: (i // bps, 0, 0))
    out = pl.pallas_call(
        _glu_ln_kernel,
        grid=(nt // _TM,),
        in_specs=[row_spec, row_spec, _const_spec((_D, 2 * _D)), _const_spec((1, 2 * _D)),
                  seq_spec, _const_spec((1, _D)), _const_spec((1, _D))],
        out_specs=row_spec,
        out_shape=jax.ShapeDtypeStruct((nt, _D), _f32),
        compiler_params=_params(("arbitrary",)),
        name="s5_glu_ln",
    )(y, x.reshape(nt, _D), w_glu, b_glu.reshape(1, 2 * _D), gate.reshape(bsz, 1, _D),
      ln_g.reshape(1, _D), ln_b.reshape(1, _D))
    return out.reshape(bsz, slen, _D)


def _ffn_kernel(xp_ref, x_ref, xn_ref, sc_ref, sh_ref, gate_ref, win_ref, bin_ref, cw_ref, cb_ref,
                wout_ref, bout_ref, lng_ref, lnb_ref, o_ref, *, blocks_per_seq):
    i = pl.program_id(0)
    first = (i % blocks_per_seq) == 0
    last = (i % blocks_per_seq) == blocks_per_seq - 1
    halo = _BF16_ROWS
    rows = _TM + 2 * halo

    sc1 = 1.0 + sc_ref[...]
    sh = sh_ref[...]
    x = x_ref[...]
    h = (x * sc1 + sh).astype(_bf16)
    h_prev = (xp_ref[...] * sc1 + sh).astype(_bf16)
    h_next = (xn_ref[...] * sc1 + sh).astype(_bf16)
    h_ext = jnp.concatenate([h_prev, h, h_next], axis=0)

    rid = lax.broadcasted_iota(jnp.int32, (rows, 1), 0)
    pad = (first & (rid == halo - 1)) | (last & (rid == halo + _TM))

    y = jnp.zeros((_TM, _D), _f32)
    for lo, hi in zip(_FFN_SPLIT[:-1], _FFN_SPLIT[1:]):
        u = _dot(h_ext, win_ref[:, lo:hi]) + bin_ref[:, lo:hi]
        u = jnp.where(pad, 0.0, u)
        g = _dot(h, win_ref[:, _D_FF + lo:_D_FF + hi]) + bin_ref[:, _D_FF + lo:_D_FF + hi]
        up = pltpu.roll(u, 1, axis=0)[halo:halo + _TM]
        dn = pltpu.roll(u, rows - 1, axis=0)[halo:halo + _TM]
        conv = (up * cw_ref[0:1, lo:hi] + u[halo:halo + _TM] * cw_ref[1:2, lo:hi]
                + dn * cw_ref[2:3, lo:hi] + cb_ref[:, lo:hi])
        act = (jax.nn.gelu(conv) * g).astype(_bf16)
        y = y + _dot(act, wout_ref[lo:hi, :])
    y = y + bout_ref[...]
    z = _ALPHA * x + gate_ref[...] * y
    o_ref[...] = _layer_norm(z, lng_ref[...], lnb_ref[...])


def _ffn_layer(x, sc, sh, gate, w_in, b_in, conv_w, conv_b, w_out, b_out, ln_g, ln_b):
    bsz, slen, _ = x.shape
    nt = bsz * slen
    bps = slen // _TM
    hb = _TM // _BF16_ROWS
    n_halo = nt // _BF16_ROWS
    row_spec = pl.BlockSpec((_TM, _D), lambda i: (i, 0))
    prev_spec = pl.BlockSpec((_BF16_ROWS, _D), lambda i: (jnp.maximum(i * hb - 1, 0), 0))
    next_spec = pl.BlockSpec((_BF16_ROWS, _D), lambda i: (jnp.minimum((i + 1) * hb, n_halo - 1), 0))
    seq_spec = pl.BlockSpec((None, 1, _D), lambda i: (i // bps, 0, 0))
    x2 = x.reshape(nt, _D)
    out = pl.pallas_call(
        functools.partial(_ffn_kernel, blocks_per_seq=bps),
        grid=(nt // _TM,),
        in_specs=[prev_spec, row_spec, next_spec, seq_spec, seq_spec, seq_spec,
                  _const_spec((_D, 2 * _D_FF)), _const_spec((1, 2 * _D_FF)),
                  _const_spec((3, _D_FF)), _const_spec((1, _D_FF)),
                  _const_spec((_D_FF, _D)), _const_spec((1, _D)),
                  _const_spec((1, _D)), _const_spec((1, _D))],
        out_specs=row_spec,
        out_shape=jax.ShapeDtypeStruct((nt, _D), _f32),
        compiler_params=_params(("arbitrary",)),
        name="conv_ffn_ln",
    )(x2, x2, x2, sc.reshape(bsz, 1, _D), sh.reshape(bsz, 1, _D), gate.reshape(bsz, 1, _D),
      w_in, b_in.reshape(1, 2 * _D_FF), conv_w, conv_b.reshape(1, _D_FF),
      w_out, b_out.reshape(1, _D), ln_g.reshape(1, _D), ln_b.reshape(1, _D))
    return out.reshape(bsz, slen, _D)


def _qkv_kernel(x_ref, sc_ref, sh_ref, w_ref, b_ref, q_ref, k_ref, v_ref):
    h = (x_ref[...] * (1.0 + sc_ref[...]) + sh_ref[...]).astype(_bf16)
    qkv = _dot(h, w_ref[...]) + b_ref[...]
    q_ref[...] = (qkv[:, :_D] * (_HEAD_DIM ** -0.5)).astype(_bf16)
    k_ref[...] = qkv[:, _D:2 * _D].astype(_bf16)
    v_ref[...] = qkv[:, 2 * _D:].astype(_bf16)


def _bias_table(rpb):
    col = np.arange(_GRID_W)
    start = np.clip(col - _WIN_W // 2, 0, _GRID_W - _WIN_W)
    kc = col[None, :]
    inside = (kc >= start[:, None]) & (kc < start[:, None] + _WIN_W)
    off = np.clip(kc - col[:, None] + (_WIN_W - 1), 0, 2 * _WIN_W - 2)
    t2 = jnp.where(jnp.asarray(inside)[None, None], rpb[:, :, off], _NEG)
    return jnp.concatenate([t2[:, :-1], t2[:, 1:]], axis=-1)


def _attn_kernel(q_ref, kp_ref, kc_ref, kn_ref, vp_ref, vc_ref, vn_ref, t_ref, x_ref, gate_ref,
                 wo_ref, bo_ref, lng_ref, lnb_ref, o_ref, kcat, vcat, att, *, grid_rows):
    jb = pl.program_id(1)
    blk = _Q_ROWS * _GRID_W
    for n, (kr, vr) in enumerate(((kp_ref, vp_ref), (kc_ref, vc_ref), (kn_ref, vn_ref))):
        kcat[n * blk:(n + 1) * blk, :] = kr[...]
        vcat[n * blk:(n + 1) * blk, :] = vr[...]

    lane = lax.broadcasted_iota(jnp.int32, (_GRID_W, 2 * _HEAD_DIM), 1)
    low = lane < _HEAD_DIM

    def row_body(i, carry):
        r = _Q_ROWS * jb + i
        rs = jnp.clip(r - _WIN_H // 2, 0, grid_rows - _WIN_H)
        off = pl.multiple_of((rs - _Q_ROWS * jb + _Q_ROWS) * _GRID_W, _GRID_W)
        ro0 = rs - r + (_WIN_H - 1)
        qrow = pl.multiple_of(i * _GRID_W, _GRID_W)
        for hp in range(_N_HEADS // 2):
            cols = slice(hp * 2 * _HEAD_DIM, (hp + 1) * 2 * _HEAD_DIM)
            q2 = q_ref[pl.ds(qrow, _GRID_W), cols]
            k2 = kcat[pl.ds(off, _BAND), cols]
            v2 = vcat[pl.ds(off, _BAND), cols]
            outs = []
            for e in range(2):
                qe = jnp.where(low if e == 0 else ~low, q2, jnp.zeros_like(q2))
                s = lax.dot_general(qe, k2, (((1,), (1,)), ((), ())), preferred_element_type=_f32)
                bias = jnp.concatenate([t_ref[2 * hp + e, ro0 + 2 * m] for m in range(_WIN_H // 2)], axis=-1)
                s = s + bias
                p = jnp.exp(s - jnp.max(s, axis=-1, keepdims=True))
                denom = jnp.sum(p, axis=-1, keepdims=True)
                outs.append(_dot(p.astype(_bf16), v2) / denom)
            att[pl.ds(qrow, _GRID_W), cols] = jnp.where(low, outs[0], outs[1])
        return carry

    lax.fori_loop(0, _Q_ROWS, row_body, 0)

    y = _dot(att[...].astype(_bf16), wo_ref[...]) + bo_ref[...]
    z = _ALPHA * x_ref[...] + gate_ref[...] * y
    o_ref[...] = _layer_norm(z, lng_ref[...], lnb_ref[...])


def _attn_layer(x, sc, sh, gate, w_qkv, b_qkv, rpb, w_o, b_o, ln_g, ln_b):
    bsz, slen, _ = x.shape
    nt = bsz * slen
    bps = slen // _TM
    x2 = x.reshape(nt, _D)
    row_spec = pl.BlockSpec((_TM, _D), lambda i: (i, 0))
    seq_spec = pl.BlockSpec((None, 1, _D), lambda i: (i // bps, 0, 0))
    qkv_shape = jax.ShapeDtypeStruct((nt, _D), _bf16)
    q, k, v = pl.pallas_call(
        _qkv_kernel,
        grid=(nt // _TM,),
        in_specs=[row_spec, seq_spec, seq_spec, _const_spec((_D, 3 * _D)), _const_spec((1, 3 * _D))],
        out_specs=[row_spec] * 3,
        out_shape=[qkv_shape] * 3,
        compiler_params=_params(("arbitrary",)),
        name="attn_qkv",
    )(x2, sc.reshape(bsz, 1, _D), sh.reshape(bsz, 1, _D), w_qkv, b_qkv.reshape(1, 3 * _D))

    grid_rows = slen // _GRID_W
    nqb = grid_rows // _Q_ROWS
    blk = _Q_ROWS * _GRID_W
    cur = pl.BlockSpec((blk, _D), lambda b, j: (b * nqb + j, 0))
    prev = pl.BlockSpec((blk, _D), lambda b, j: (b * nqb + jnp.maximum(j - 1, 0), 0))
    nxt = pl.BlockSpec((blk, _D), lambda b, j: (b * nqb + jnp.minimum(j + 1, nqb - 1), 0))
    bseq = pl.BlockSpec((None, 1, _D), lambda b, j: (b, 0, 0))
    table = _bias_table(rpb)
    out = pl.pallas_call(
        functools.partial(_attn_kernel, grid_rows=grid_rows),
        grid=(bsz, nqb),
        in_specs=[cur, prev, cur, nxt, prev, cur, nxt, _const_spec(table.shape), cur, bseq,
                  _const_spec((_D, _D)), _const_spec((1, _D)), _const_spec((1, _D)), _const_spec((1, _D))],
        out_specs=cur,
        out_shape=jax.ShapeDtypeStruct((nt, _D), _f32),
        scratch_shapes=[pltpu.VMEM((3 * blk, _D), _bf16), pltpu.VMEM((3 * blk, _D), _bf16),
                        pltpu.VMEM((blk, _D), _f32)],
        compiler_params=_params(("arbitrary", "arbitrary")),
        name="nbr_attn_ln",
    )(q, k, k, k, v, v, v, table, x2, gate.reshape(bsz, 1, _D), w_o, b_o.reshape(1, _D),
      ln_g.reshape(1, _D), ln_b.reshape(1, _D))
    return out.reshape(bsz, slen, _D)


def _trunk(x, mod, s5_tables, s5_d, w_glu, b_glu, w_qkv, b_qkv, rpb, w_o, b_o,
           ffn_w_in, ffn_b_in, ffn_conv_w, ffn_conv_b, ffn_w_out, ffn_b_out, ln_g, ln_b):
    for i in range(_DEPTH):
        sh_m, sc_m, g_m, sh_f, sc_f, g_f = [mod[i, :, n * _D:(n + 1) * _D] for n in range(6)]
        if i % 2 == 0:
            x = _s5_layer(x, sc_m, sh_m, g_m, s5_tables, s5_d, w_glu, b_glu, ln_g[i, 0], ln_b[i, 0])
        else:
            x = _attn_layer(x, sc_m, sh_m, g_m, w_qkv, b_qkv, rpb, w_o, b_o, ln_g[i, 0], ln_b[i, 0])
        x = _ffn_layer(x, sc_f, sh_f, g_f, ffn_w_in[i], ffn_b_in[i], ffn_conv_w[i], ffn_conv_b[i],
                       ffn_w_out[i], ffn_b_out[i], ln_g[i, 1], ln_b[i, 1])
    return x


def kernel(x_prompt, x_sample, c_prompt, c_sample, w_ada, b_ada, ln_g, ln_b, s5_lam_re, s5_lam_im, s5_log_dt, s5_b_re, s5_b_im, s5_c_re, s5_c_im, s5_d, s5_w_glu, s5_b_glu, na_w_qkv, na_b_qkv, na_rpb, na_w_o, na_b_o, ffn_w_in, ffn_b_in, ffn_conv_w, ffn_conv_b, ffn_w_out, ffn_b_out):
    assert _DEPTH == 2 and x_prompt.shape[-1] == _D
    nb_p, nb_s = c_prompt.shape[0], c_sample.shape[0]
    nb_pad = -(-(nb_p + nb_s) // _SUBLANES) * _SUBLANES
    c_all = jnp.zeros((nb_pad, _D), _f32).at[:nb_p].set(c_prompt).at[nb_p:nb_p + nb_s].set(c_sample)
    mod = _modulation(c_all, w_ada, b_ada)

    tables = _s5_tables(s5_lam_re[0], s5_lam_im[0], s5_log_dt[0], s5_b_re[0], s5_b_im[0],
                        s5_c_re[0], s5_c_im[0])
    shared = (tables, s5_d[0], s5_w_glu[0].astype(_bf16), s5_b_glu[0],
              na_w_qkv[0].astype(_bf16), na_b_qkv[0], na_rpb[0], na_w_o[0].astype(_bf16), na_b_o[0],
              ffn_w_in.astype(_bf16), ffn_b_in, ffn_conv_w, ffn_conv_b, ffn_w_out.astype(_bf16), ffn_b_out,
              ln_g, ln_b)
    y_prompt = _trunk(x_prompt, mod[:, :nb_p], *shared)
    y_sample = _trunk(x_sample, mod[:, nb_p:nb_p + nb_s], *shared)
    return (y_prompt, y_sample)
```

```python
import functools

import numpy as np
import jax
import jax.numpy as jnp
from jax import lax
from jax.experimental import pallas as pl
from jax.experimental.pallas import tpu as pltpu

_D = 1024
_DEPTH = 2
_GRID_W = 64
_GROUP_DIM = 16
_N_GROUPS = _D // _GROUP_DIM
_STATE_DIM = 64
_N_HEADS = 16
_HEAD_DIM = _D // _N_HEADS
_WIN_H = 8
_WIN_W = 16
_D_FF = 2816
_ALPHA = (2 * _DEPTH) ** 0.25
_LN_EPS = 1e-5

_LANES = 128
_SUBLANES = 8
_BF16_ROWS = 16
_VMEM_LIMIT = 56 * 1024 * 1024

_L = 16
_OG = _LANES // _GROUP_DIM
_N_OCT = _N_GROUPS // _OG
_OW = _L * _LANES
_OS = _OG * _STATE_DIM
_STATE_W = _N_GROUPS * _STATE_DIM
_S5_ROWS = 256
_SCAN_ROWS = 256

_TM = 512
_FFN_SPLIT = (0, 1536, _D_FF)
_Q_ROWS = 8
_BAND = _WIN_H * _GRID_W
_NEG = -1e30

_f32 = jnp.float32
_bf16 = jnp.bfloat16


def _params(sem):
    return pltpu.CompilerParams(dimension_semantics=sem, vmem_limit_bytes=_VMEM_LIMIT)


def _const_spec(shape):
    nd = len(shape)
    return pl.BlockSpec(shape, lambda *_: (0,) * nd, pipeline_mode=pl.Buffered(1))


def _layer_norm(z, g, b):
    mu = jnp.mean(z, axis=-1, keepdims=True)
    d = z - mu
    var = jnp.mean(d * d, axis=-1, keepdims=True)
    return d * lax.rsqrt(var + _LN_EPS) * g + b


def _dot(a, b):
    return jnp.dot(a, b, preferred_element_type=_f32)


def _mod_kernel(c_ref, w_ref, b_ref, o_ref):
    cond = jax.nn.silu(c_ref[...]).astype(_bf16)
    o_ref[...] = _dot(cond, w_ref[...].astype(_bf16)) + b_ref[...]


def _modulation(c_all, w_ada, b_ada):
    nb = c_all.shape[0]
    tn = 1536
    return pl.pallas_call(
        _mod_kernel,
        grid=(_DEPTH, 6 * _D // tn),
        in_specs=[
            pl.BlockSpec((nb, _D), lambda l, n: (0, 0)),
            pl.BlockSpec((None, _D, tn), lambda l, n: (l, 0, n)),
            pl.BlockSpec((None, 1, tn), lambda l, n: (l, 0, n)),
        ],
        out_specs=pl.BlockSpec((None, nb, tn), lambda l, n: (l, 0, n)),
        out_shape=jax.ShapeDtypeStruct((_DEPTH, nb, 6 * _D), _f32),
        compiler_params=_params(("arbitrary", "arbitrary")),
        name="adaln_mod",
    )(c_all, w_ada, b_ada.reshape(_DEPTH, 1, 6 * _D))


def _s5_tables(lam_re, lam_im, log_dt, b_re, b_im, c_re, c_im):
    hi = lax.Precision.HIGHEST
    dt = jnp.exp(log_dt)[..., None]
    z_re, z_im = lam_re * dt, lam_im * dt
    mag = jnp.exp(z_re)
    ab_re, ab_im = mag * jnp.cos(z_im), mag * jnp.sin(z_im)
    den = lam_re * lam_re + lam_im * lam_im
    nr, ni = ab_re - 1.0, ab_im
    f_re = (nr * lam_re + ni * lam_im) / den
    f_im = (ni * lam_re - nr * lam_im) / den
    bb_re = f_re[..., None] * b_re - f_im[..., None] * b_im
    bb_im = f_re[..., None] * b_im + f_im[..., None] * b_re

    k = jnp.arange(_L + 1, dtype=_f32)[:, None, None, None]
    pw_mag = jnp.exp(k * z_re)
    pw_re, pw_im = pw_mag * jnp.cos(k * z_im), pw_mag * jnp.sin(k * z_im)

    abb_re = pw_re[..., None] * bb_re - pw_im[..., None] * bb_im
    abb_im = pw_re[..., None] * bb_im + pw_im[..., None] * bb_re

    eye = jnp.eye(_OG, dtype=_f32)

    lag = (jnp.einsum('kdgph,dgop->kdgho', abb_re[:_L], c_re, precision=hi)
           - jnp.einsum('kdgph,dgop->kdgho', abb_im[:_L], c_im, precision=hi))
    s_idx = np.arange(_L)[:, None]
    t_idx = np.arange(_L)[None, :]
    kf = lag[np.clip(t_idx - s_idx, 0, _L - 1), 0]
    kb = lag[np.clip(s_idx - t_idx, 0, _L - 1), 1]
    mf = jnp.asarray(t_idx >= s_idx, _f32)[:, :, None, None, None]
    mb = jnp.asarray(s_idx >= t_idx, _f32)[:, :, None, None, None]
    tp = (kf * mf + kb * mb).reshape(_L, _L, _N_OCT, _OG, _GROUP_DIM, _GROUP_DIM)
    toep = jnp.einsum('stqgho,gf->qsghtfo', tp, eye).reshape(_N_OCT, _OW, _OW)

    rev = np.arange(_L - 1, -1, -1)
    st = jnp.stack([abb_re[rev, 0], abb_im[rev, 0], abb_re[:_L, 1], abb_im[:_L, 1]], axis=0)
    st = st.reshape(4, _L, _N_OCT, _OG, _STATE_DIM, _GROUP_DIM)
    w_state = jnp.einsum('csqgph,gf->qsghcfp', st, eye).reshape(_N_OCT, _OW, 4 * _OS)

    def out_mat(d, powers):
        pr_, pi_ = pw_re[powers, d], pw_im[powers, d]
        cr, ci = c_re[d], c_im[d]
        wr = cr[None] * pr_[:, :, None, :] - ci[None] * pi_[:, :, None, :]
        wi = cr[None] * pi_[:, :, None, :] + ci[None] * pr_[:, :, None, :]
        return wr, -wi

    wfr, wfi = out_mat(0, np.arange(1, _L + 1))
    wbr, wbi = out_mat(1, np.arange(_L, 0, -1))
    wo = jnp.stack([wfr, wfi, wbr, wbi], axis=0)
    wo = wo.reshape(4, _L, _N_OCT, _OG, _GROUP_DIM, _STATE_DIM)
    w_out = jnp.einsum('ctqgop,gf->qcgptfo', wo, eye).reshape(_N_OCT, 4 * _OS, _OW)

    flow = np.stack([np.arange(_SUBLANES), np.arange(_SUBLANES - 1, -1, -1)])[:, :, None, None]

    def chunk_pow(m):
        e = jnp.asarray(m, _f32) * float(_L)
        pm = jnp.exp(e * z_re[:, None])
        return jnp.stack([pm * jnp.cos(e * z_im[:, None]), pm * jnp.sin(e * z_im[:, None])], axis=1)

    def masked(d):
        return chunk_pow(np.full_like(flow, d)) * jnp.asarray(flow >= d, _f32)[:, None]

    scan_tab = jnp.stack([chunk_pow(np.ones_like(flow)), masked(1), masked(2), masked(4), chunk_pow(flow)], axis=1)
    scan_tab = scan_tab.reshape(2, 5, 2, _SUBLANES, _STATE_W)
    return toep.astype(_bf16), w_state.astype(_bf16), w_out.astype(_bf16), scan_tab


def _chunk_tokens(x_ref, sc_ref, sh_ref):
    rows = x_ref.shape[0] // _L
    sc1 = 1.0 + sc_ref[...]
    sh = sh_ref[...]
    return [x_ref[pl.ds(s, rows, stride=_L), :] * sc1 + sh for s in range(_L)]


def _s5_state_kernel(x_ref, sc_ref, sh_ref, w_ref, *o_refs):
    us = _chunk_tokens(x_ref, sc_ref, sh_ref)
    u = jnp.concatenate([v.astype(_bf16) for v in us], axis=-1)
    r = _dot(u, w_ref[...])
    for c, o_ref in enumerate(o_refs):
        o_ref[...] = r[:, c * _OS:(c + 1) * _OS]


def _s5_scan_kernel(lfr, lfi, lbr, lbi, tab_ref, ofr, ofi, obr, obi, carry):
    @pl.when(pl.program_id(2) == 0)
    def _():
        carry[...] = jnp.zeros_like(carry)

    tiles = lfr.shape[0] // _SUBLANES
    sub = lax.broadcasted_iota(jnp.int32, (_SUBLANES, lfr.shape[1]), 0)

    def cmul(ar, ai, br, bi):
        return ar * br - ai * bi, ar * bi + ai * br

    def tile_scan(xr, xi, f0r, f0i, d):
        def shift(v, k):
            return pltpu.roll(v, k if d == 0 else _SUBLANES - k, axis=0)

        ir, ii = xr, xi
        for n, k in ((1, 1), (2, 2), (3, 4)):
            pr, pi = cmul(tab_ref[d, n, 0], tab_ref[d, n, 1], shift(ir, k), shift(ii, k))
            ir, ii = ir + pr, ii + pi
        first = sub == (0 if d == 0 else _SUBLANES - 1)
        cr, ci = cmul(tab_ref[d, 4, 0], tab_ref[d, 4, 1], f0r, f0i)
        fin_r = cr + jnp.where(first, 0.0, shift(ir, 1))
        fin_i = ci + jnp.where(first, 0.0, shift(ii, 1))
        nr, ni = cmul(tab_ref[d, 0, 0], tab_ref[d, 0, 1], fin_r, fin_i)
        nr, ni = nr + xr, ni + xi
        end = _SUBLANES - 1 if d == 0 else 0
        return (fin_r, fin_i, jnp.broadcast_to(nr[end:end + 1], nr.shape),
                jnp.broadcast_to(ni[end:end + 1], ni.shape))

    def body(i, c):
        rf = pl.multiple_of(i * _SUBLANES, _SUBLANES)
        rb = pl.multiple_of((tiles - 1 - i) * _SUBLANES, _SUBLANES)
        fr, fi, c0, c1 = tile_scan(lfr[pl.ds(rf, _SUBLANES), :], lfi[pl.ds(rf, _SUBLANES), :], c[0], c[1], 0)
        ofr[pl.ds(rf, _SUBLANES), :] = fr
        ofi[pl.ds(rf, _SUBLANES), :] = fi
        br, bi, c2, c3 = tile_scan(lbr[pl.ds(rb, _SUBLANES), :], lbi[pl.ds(rb, _SUBLANES), :], c[2], c[3], 1)
        obr[pl.ds(rb, _SUBLANES), :] = br
        obi[pl.ds(rb, _SUBLANES), :] = bi
        return (c0, c1, c2, c3)

    c = lax.fori_loop(0, tiles, body, (carry[0], carry[1], carry[2], carry[3]))
    for q in range(4):
        carry[q] = c[q]


def _s5_out_kernel(x_ref, sc_ref, sh_ref, d_ref, sfr, sfi, sbr, sbi, t_ref, w_ref, o_ref):
    s_refs = (sfr, sfi, sbr, sbi)
    rows = x_ref.shape[0] // _L
    d = d_ref[...]
    us = _chunk_tokens(x_ref, sc_ref, sh_ref)
    u = jnp.concatenate([v.astype(_bf16) for v in us], axis=-1)
    st = jnp.concatenate([s[...] for s in s_refs], axis=-1).astype(_bf16)
    y = _dot(u, t_ref[...]) + _dot(st, w_ref[...])
    for t in range(_L):
        y_t = d * us[t] + y[:, t * _LANES:(t + 1) * _LANES]
        o_ref[pl.ds(t, rows, stride=_L), :] = jax.nn.gelu(y_t)


def _glu_ln_kernel(y_ref, x_ref, w_ref, b_ref, gate_ref, lng_ref, lnb_ref, o_ref):
    ag = _dot(y_ref[...].astype(_bf16), w_ref[...]) + b_ref[...]
    mix = ag[:, :_D] * jax.nn.sigmoid(ag[:, _D:])
    z = _ALPHA * x_ref[...] + gate_ref[...] * mix
    o_ref[...] = _layer_norm(z, lng_ref[...], lnb_ref[...])


def _s5_layer(x, sc, sh, gate, tables, d_skip, w_glu, b_glu, ln_g, ln_b):
    bsz, slen, _ = x.shape
    cs = slen // _L
    nt = bsz * slen
    toep, w_state, w_out, scan_tab = tables
    rows = min(cs, _S5_ROWS)
    bpc = cs // rows
    nblk = bsz * bpc
    x2 = x.reshape(nt, _D)
    sc3, sh3 = sc.reshape(bsz, 1, _D), sh.reshape(bsz, 1, _D)

    x_spec = pl.BlockSpec((rows * _L, _LANES), lambda q, i: (i, q))
    mod_spec = pl.BlockSpec((None, 1, _LANES), lambda q, i: (i // bpc, 0, q))
    st_spec = pl.BlockSpec((None, rows, _OS), lambda q, i: (i // bpc, i % bpc, q))
    st_shape = jax.ShapeDtypeStruct((bsz, cs, _STATE_W), _f32)

    def slab_spec(k, n):
        return pl.BlockSpec((None, k, n), lambda q, i: (q, 0, 0), pipeline_mode=pl.Buffered(1))

    loc = pl.pallas_call(
        _s5_state_kernel,
        grid=(_N_OCT, nblk),
        in_specs=[x_spec, mod_spec, mod_spec, slab_spec(_OW, 4 * _OS)],
        out_specs=[st_spec] * 4,
        out_shape=[st_shape] * 4,
        compiler_params=_params(("arbitrary", "arbitrary")),
        name="s5_chunk_states",
    )(x2, sc3, sh3, w_state)

    cblk = min(cs, _SCAN_ROWS)
    nsb = cs // cblk
    fwd_spec = pl.BlockSpec((None, cblk, _OS), lambda b, q, j: (b, j, q))
    bwd_spec = pl.BlockSpec((None, cblk, _OS), lambda b, q, j: (b, nsb - 1 - j, q))
    sin = pl.pallas_call(
        _s5_scan_kernel,
        grid=(bsz, _N_OCT, nsb),
        in_specs=[fwd_spec, fwd_spec, bwd_spec, bwd_spec,
                  pl.BlockSpec((2, 5, 2, _SUBLANES, _OS), lambda b, q, j: (0, 0, 0, 0, q))],
        out_specs=[fwd_spec, fwd_spec, bwd_spec, bwd_spec],
        out_shape=[st_shape] * 4,
        scratch_shapes=[pltpu.VMEM((4, _SUBLANES, _OS), _f32)],
        compiler_params=_params(("arbitrary", "arbitrary", "arbitrary")),
        name="s5_chunk_scan",
    )(*loc, scan_tab)

    y = pl.pallas_call(
        _s5_out_kernel,
        grid=(_N_OCT, nblk),
        in_specs=[x_spec, mod_spec, mod_spec, pl.BlockSpec((1, _LANES), lambda q, i: (0, q)),
                  st_spec, st_spec, st_spec, st_spec, slab_spec(_OW, _OW), slab_spec(4 * _OS, _OW)],
        out_specs=x_spec,
        out_shape=jax.ShapeDtypeStruct((nt, _D), _f32),
        compiler_params=_params(("arbitrary", "arbitrary")),
        name="s5_chunk_out",
    )(x2, sc3, sh3, d_skip.reshape(1, _D), *sin, toep, w_out)

    bps = slen // _TM
    row_spec = pl.BlockSpec((_TM, _D), lambda i: (i, 0))
    seq_spec = pl.BlockSpec((None, 1, _D), lambda i: (i // bps, 0, 0))
    out = pl.pallas_call(
        _glu_ln_kernel,
        grid=(nt // _TM,),
        in_specs=[row_spec, row_spec, _const_spec((_D, 2 * _D)), _const_spec((1, 2 * _D)),
                  seq_spec, _const_spec((1, _D)), _const_spec((1, _D))],
        out_specs=row_spec,
        out_shape=jax.ShapeDtypeStruct((nt, _D), _f32),
        compiler_params=_params(("arbitrary",)),
        name="s5_glu_ln",
    )(y, x2, w_glu, b_glu.reshape(1, 2 * _D), gate.reshape(bsz, 1, _D),
      ln_g.reshape(1, _D), ln_b.reshape(1, _D))
    return out.reshape(bsz, slen, _D)


def _ffn_kernel(xp_ref, x_ref, xn_ref, sc_ref, sh_ref, gate_ref, win_ref, bin_ref, cw_ref, cb_ref,
                wout_ref, bout_ref, lng_ref, lnb_ref, o_ref, *, blocks_per_seq):
    i = pl.program_id(0)
    first = (i % blocks_per_seq) == 0
    last = (i % blocks_per_seq) == blocks_per_seq - 1
    halo = _BF16_ROWS
    rows = _TM + 2 * halo

    sc1 = 1.0 + sc_ref[...]
    sh = sh_ref[...]
    x = x_ref[...]
    h = (x * sc1 + sh).astype(_bf16)
    h_prev = (xp_ref[...] * sc1 + sh).astype(_bf16)
    h_next = (xn_ref[...] * sc1 + sh).astype(_bf16)
    h_ext = jnp.concatenate([h_prev, h, h_next], axis=0)

    rid = lax.broadcasted_iota(jnp.int32, (rows, 1), 0)
    pad = (first & (rid == halo - 1)) | (last & (rid == halo + _TM))

    y = jnp.zeros((_TM, _D), _f32)
    for lo, hi in zip(_FFN_SPLIT[:-1], _FFN_SPLIT[1:]):
        u = _dot(h_ext, win_ref[:, lo:hi]) + bin_ref[:, lo:hi]
        u = jnp.where(pad, 0.0, u)
        g = _dot(h, win_ref[:, _D_FF + lo:_D_FF + hi]) + bin_ref[:, _D_FF + lo:_D_FF + hi]
        up = pltpu.roll(u, 1, axis=0)[halo:halo + _TM]
        dn = pltpu.roll(u, rows - 1, axis=0)[halo:halo + _TM]
        conv = (up * cw_ref[0:1, lo:hi] + u[halo:halo + _TM] * cw_ref[1:2, lo:hi]
                + dn * cw_ref[2:3, lo:hi] + cb_ref[:, lo:hi])
        act = (jax.nn.gelu(conv) * g).astype(_bf16)
        y = y + _dot(act, wout_ref[lo:hi, :])
    y = y + bout_ref[...]
    z = _ALPHA * x + gate_ref[...] * y
    o_ref[...] = _layer_norm(z, lng_ref[...], lnb_ref[...])


def _ffn_layer(x, sc, sh, gate, w_in, b_in, conv_w, conv_b, w_out, b_out, ln_g, ln_b):
    bsz, slen, _ = x.shape
    nt = bsz * slen
    bps = slen // _TM
    hb = _TM // _BF16_ROWS
    n_halo = nt // _BF16_ROWS
    row_spec = pl.BlockSpec((_TM, _D), lambda i: (i, 0))
    prev_spec = pl.BlockSpec((_BF16_ROWS, _D), lambda i: (jnp.maximum(i * hb - 1, 0), 0))
    next_spec = pl.BlockSpec((_BF16_ROWS, _D), lambda i: (jnp.minimum((i + 1) * hb, n_halo - 1), 0))
    seq_spec = pl.BlockSpec((None, 1, _D), lambda i: (i // bps, 0, 0))
    x2 = x.reshape(nt, _D)
    out = pl.pallas_call(
        functools.partial(_ffn_kernel, blocks_per_seq=bps),
        grid=(nt // _TM,),
        in_specs=[prev_spec, row_spec, next_spec, seq_spec, seq_spec, seq_spec,
                  _const_spec((_D, 2 * _D_FF)), _const_spec((1, 2 * _D_FF)),
                  _const_spec((3, _D_FF)), _const_spec((1, _D_FF)),
                  _const_spec((_D_FF, _D)), _const_spec((1, _D)),
                  _const_spec((1, _D)), _const_spec((1, _D))],
        out_specs=row_spec,
        out_shape=jax.ShapeDtypeStruct((nt, _D), _f32),
        compiler_params=_params(("arbitrary",)),
        name="conv_ffn_ln",
    )(x2, x2, x2, sc.reshape(bsz, 1, _D), sh.reshape(bsz, 1, _D), gate.reshape(bsz, 1, _D),
      w_in, b_in.reshape(1, 2 * _D_FF), conv_w, conv_b.reshape(1, _D_FF),
      w_out, b_out.reshape(1, _D), ln_g.reshape(1, _D), ln_b.reshape(1, _D))
    return out.reshape(bsz, slen, _D)


def _qkv_kernel(x_ref, sc_ref, sh_ref, w_ref, b_ref, q_ref, k_ref, v_ref):
    h = (x_ref[...] * (1.0 + sc_ref[...]) + sh_ref[...]).astype(_bf16)
    qkv = _dot(h, w_ref[...]) + b_ref[...]
    q_ref[...] = (qkv[:, :_D] * (_HEAD_DIM ** -0.5)).astype(_bf16)
    k_ref[...] = qkv[:, _D:2 * _D].astype(_bf16)
    v_ref[...] = qkv[:, 2 * _D:].astype(_bf16)


def _bias_table(rpb):
    col = np.arange(_GRID_W)
    start = np.clip(col - _WIN_W // 2, 0, _GRID_W - _WIN_W)
    kc = col[None, :]
    inside = (kc >= start[:, None]) & (kc < start[:, None] + _WIN_W)
    off = np.clip(kc - col[:, None] + (_WIN_W - 1), 0, 2 * _WIN_W - 2)
    t2 = jnp.where(jnp.asarray(inside)[None, None], rpb[:, :, off], _NEG)
    return jnp.concatenate([t2[:, :-1], t2[:, 1:]], axis=-1)


def _attn_kernel(q_ref, kp_ref, kc_ref, kn_ref, vp_ref, vc_ref, vn_ref, t_ref, x_ref, gate_ref,
                 wo_ref, bo_ref, lng_ref, lnb_ref, o_ref, kcat, vcat, att, *, grid_rows):
    jb = pl.program_id(1)
    blk = _Q_ROWS * _GRID_W
    for n, (kr, vr) in enumerate(((kp_ref, vp_ref), (kc_ref, vc_ref), (kn_ref, vn_ref))):
        kcat[n * blk:(n + 1) * blk, :] = kr[...]
        vcat[n * blk:(n + 1) * blk, :] = vr[...]

    lane = lax.broadcasted_iota(jnp.int32, (_GRID_W, 2 * _HEAD_DIM), 1)
    low = lane < _HEAD_DIM
    n_pairs = _N_HEADS // 2

    def row_body(i, carry):
        r = _Q_ROWS * jb + i
        rs = jnp.clip(r - _WIN_H // 2, 0, grid_rows - _WIN_H)
        off = pl.multiple_of((rs - _Q_ROWS * jb + _Q_ROWS) * _GRID_W, _GRID_W)
        ro0 = rs - r + (_WIN_H - 1)
        qrow = pl.multiple_of(i * _GRID_W, _GRID_W)
        cols = [slice(hp * 2 * _HEAD_DIM, (hp + 1) * 2 * _HEAD_DIM) for hp in range(n_pairs)]
        scores = []
        for hp in range(n_pairs):
            q2 = q_ref[pl.ds(qrow, _GRID_W), cols[hp]]
            k2 = kcat[pl.ds(off, _BAND), cols[hp]]
            for e in range(2):
                qe = jnp.where(low if e == 0 else ~low, q2, jnp.zeros_like(q2))
                scores.append(lax.dot_general(qe, k2, (((1,), (1,)), ((), ())), preferred_element_type=_f32))
        probs, denoms = [], []
        for h, s in enumerate(scores):
            bias = jnp.concatenate([t_ref[h, ro0 + 2 * m] for m in range(_WIN_H // 2)], axis=-1)
            s = s + bias
            p = jnp.exp(s - jnp.max(s, axis=-1, keepdims=True))
            denoms.append(jnp.sum(p, axis=-1, keepdims=True))
            probs.append(p.astype(_bf16))
        for hp in range(n_pairs):
            v2 = vcat[pl.ds(off, _BAND), cols[hp]]
            o0 = _dot(probs[2 * hp], v2) / denoms[2 * hp]
            o1 = _dot(probs[2 * hp + 1], v2) / denoms[2 * hp + 1]
            att[pl.ds(qrow, _GRID_W), cols[hp]] = jnp.where(low, o0, o1)
        return carry

    lax.fori_loop(0, _Q_ROWS, row_body, 0)

    y = _dot(att[...].astype(_bf16), wo_ref[...]) + bo_ref[...]
    z = _ALPHA * x_ref[...] + gate_ref[...] * y
    o_ref[...] = _layer_norm(z, lng_ref[...], lnb_ref[...])


def _attn_layer(x, sc, sh, gate, w_qkv, b_qkv, rpb, w_o, b_o, ln_g, ln_b):
    bsz, slen, _ = x.shape
    nt = bsz * slen
    bps = slen // _TM
    x2 = x.reshape(nt, _D)
    row_spec = pl.BlockSpec((_TM, _D), lambda i: (i, 0))
    seq_spec = pl.BlockSpec((None, 1, _D), lambda i: (i // bps, 0, 0))
    qkv_shape = jax.ShapeDtypeStruct((nt, _D), _bf16)
    q, k, v = pl.pallas_call(
        _qkv_kernel,
        grid=(nt // _TM,),
        in_specs=[row_spec, seq_spec, seq_spec, _const_spec((_D, 3 * _D)), _const_spec((1, 3 * _D))],
        out_specs=[row_spec] * 3,
        out_shape=[qkv_shape] * 3,
        compiler_params=_params(("arbitrary",)),
        name="attn_qkv",
    )(x2, sc.reshape(bsz, 1, _D), sh.reshape(bsz, 1, _D), w_qkv, b_qkv.reshape(1, 3 * _D))

    grid_rows = slen // _GRID_W
    nqb = grid_rows // _Q_ROWS
    blk = _Q_ROWS * _GRID_W
    cur = pl.BlockSpec((blk, _D), lambda b, j: (b * nqb + j, 0))
    prev = pl.BlockSpec((blk, _D), lambda b, j: (b * nqb + jnp.maximum(j - 1, 0), 0))
    nxt = pl.BlockSpec((blk, _D), lambda b, j: (b * nqb + jnp.minimum(j + 1, nqb - 1), 0))
    bseq = pl.BlockSpec((None, 1, _D), lambda b, j: (b, 0, 0))
    table = _bias_table(rpb)
    out = pl.pallas_call(
        functools.partial(_attn_kernel, grid_rows=grid_rows),
        grid=(bsz, nqb),
        in_specs=[cur, prev, cur, nxt, prev, cur, nxt, _const_spec(table.shape), cur, bseq,
                  _const_spec((_D, _D)), _const_spec((1, _D)), _const_spec((1, _D)), _const_spec((1, _D))],
        out_specs=cur,
        out_shape=jax.ShapeDtypeStruct((nt, _D), _f32),
        scratch_shapes=[pltpu.VMEM((3 * blk, _D), _bf16), pltpu.VMEM((3 * blk, _D), _bf16),
                        pltpu.VMEM((blk, _D), _f32)],
        compiler_params=_params(("arbitrary", "arbitrary")),
        name="nbr_attn_ln",
    )(q, k, k, k, v, v, v, table, x2, gate.reshape(bsz, 1, _D), w_o, b_o.reshape(1, _D),
      ln_g.reshape(1, _D), ln_b.reshape(1, _D))
    return out.reshape(bsz, slen, _D)


def _trunk(x, mod, s5_tables, s5_d, w_glu, b_glu, w_qkv, b_qkv, rpb, w_o, b_o,
           ffn_w_in, ffn_b_in, ffn_conv_w, ffn_conv_b, ffn_w_out, ffn_b_out, ln_g, ln_b):
    for i in range(_DEPTH):
        sh_m, sc_m, g_m, sh_f, sc_f, g_f = [mod[i, :, n * _D:(n + 1) * _D] for n in range(6)]
        if i % 2 == 0:
            x = _s5_layer(x, sc_m, sh_m, g_m, s5_tables, s5_d, w_glu, b_glu, ln_g[i, 0], ln_b[i, 0])
        else:
            x = _attn_layer(x, sc_m, sh_m, g_m, w_qkv, b_qkv, rpb, w_o, b_o, ln_g[i, 0], ln_b[i, 0])
        x = _ffn_layer(x, sc_f, sh_f, g_f, ffn_w_in[i], ffn_b_in[i], ffn_conv_w[i], ffn_conv_b[i],
                       ffn_w_out[i], ffn_b_out[i], ln_g[i, 1], ln_b[i, 1])
    return x


def kernel(x_prompt, x_sample, c_prompt, c_sample, w_ada, b_ada, ln_g, ln_b, s5_lam_re, s5_lam_im, s5_log_dt, s5_b_re, s5_b_im, s5_c_re, s5_c_im, s5_d, s5_w_glu, s5_b_glu, na_w_qkv, na_b_qkv, na_rpb, na_w_o, na_b_o, ffn_w_in, ffn_b_in, ffn_conv_w, ffn_conv_b, ffn_w_out, ffn_b_out):
    assert _DEPTH == 2 and x_prompt.shape[-1] == _D
    nb_p, nb_s = c_prompt.shape[0], c_sample.shape[0]
    nb_pad = -(-(nb_p + nb_s) // _SUBLANES) * _SUBLANES
    c_all = jnp.zeros((nb_pad, _D), _f32).at[:nb_p].set(c_prompt).at[nb_p:nb_p + nb_s].set(c_sample)
    mod = _modulation(c_all, w_ada, b_ada)

    tables = _s5_tables(s5_lam_re[0], s5_lam_im[0], s5_log_dt[0], s5_b_re[0], s5_b_im[0],
                        s5_c_re[0], s5_c_im[0])
    shared = (tables, s5_d[0], s5_w_glu[0].astype(_bf16), s5_b_glu[0],
              na_w_qkv[0].astype(_bf16), na_b_qkv[0], na_rpb[0], na_w_o[0].astype(_bf16), na_b_o[0],
              ffn_w_in.astype(_bf16), ffn_b_in, ffn_conv_w, ffn_conv_b, ffn_w_out.astype(_bf16), ffn_b_out,
              ln_g, ln_b)
    y_prompt = _trunk(x_prompt, mod[:, :nb_p], *shared)
    y_sample = _trunk(x_sample, mod[:, nb_p:nb_p + nb_s], *shared)
    return (y_prompt, y_sample)
```

```python
import functools

import numpy as np
import jax
import jax.numpy as jnp
from jax import lax
from jax.experimental import pallas as pl
from jax.experimental.pallas import tpu as pltpu

_D = 1024
_DEPTH = 2
_GRID_W = 64
_GROUP_DIM = 16
_N_GROUPS = _D // _GROUP_DIM
_STATE_DIM = 64
_N_HEADS = 16
_HEAD_DIM = _D // _N_HEADS
_WIN_H = 8
_WIN_W = 16
_D_FF = 2816
_ALPHA = (2 * _DEPTH) ** 0.25
_LN_EPS = 1e-5

_LANES = 128
_SUBLANES = 8
_BF16_ROWS = 16
_VMEM_LIMIT = 56 * 1024 * 1024

_L = 16
_GW = _L * _GROUP_DIM
_SG = 4
_SLAB_LANES = _SG * _GROUP_DIM
_N_SLABS = _N_GROUPS // _SG
_N_TILES = _D // _LANES
_CW = _L * _SLAB_LANES
_CS = _SG * _STATE_DIM
_TS = 2 * _CS
_STATE_W = _N_GROUPS * _STATE_DIM
_S5_ROWS = 256
_SCAN_ROWS = 256

_TM = 512
_FFN_SPLIT = (0, 1536, _D_FF)
_Q_ROWS = 8
_BAND = _WIN_H * _GRID_W
_NEG = -1e30

_f32 = jnp.float32
_bf16 = jnp.bfloat16


def _params(sem):
    return pltpu.CompilerParams(dimension_semantics=sem, vmem_limit_bytes=_VMEM_LIMIT)


def _const_spec(shape):
    nd = len(shape)
    return pl.BlockSpec(shape, lambda *_: (0,) * nd, pipeline_mode=pl.Buffered(1))


def _layer_norm(z, g, b):
    mu = jnp.mean(z, axis=-1, keepdims=True)
    d = z - mu
    var = jnp.mean(d * d, axis=-1, keepdims=True)
    return d * lax.rsqrt(var + _LN_EPS) * g + b


def _dot(a, b):
    return jnp.dot(a, b, preferred_element_type=_f32)


def _mod_kernel(c_ref, w_ref, b_ref, o_ref):
    cond = jax.nn.silu(c_ref[...]).astype(_bf16)
    o_ref[...] = _dot(cond, w_ref[...].astype(_bf16)) + b_ref[...]


def _modulation(c_all, w_ada, b_ada):
    nb = c_all.shape[0]
    tn = 1536
    return pl.pallas_call(
        _mod_kernel,
        grid=(_DEPTH, 6 * _D // tn),
        in_specs=[
            pl.BlockSpec((nb, _D), lambda l, n: (0, 0)),
            pl.BlockSpec((None, _D, tn), lambda l, n: (l, 0, n)),
            pl.BlockSpec((None, 1, tn), lambda l, n: (l, 0, n)),
        ],
        out_specs=pl.BlockSpec((None, nb, tn), lambda l, n: (l, 0, n)),
        out_shape=jax.ShapeDtypeStruct((_DEPTH, nb, 6 * _D), _f32),
        compiler_params=_params(("arbitrary", "arbitrary")),
        name="adaln_mod",
    )(c_all, w_ada, b_ada.reshape(_DEPTH, 1, 6 * _D))


def _s5_tables(lam_re, lam_im, log_dt, b_re, b_im, c_re, c_im):
    hi = lax.Precision.HIGHEST
    dt = jnp.exp(log_dt)[..., None]
    z_re, z_im = lam_re * dt, lam_im * dt
    mag = jnp.exp(z_re)
    ab_re, ab_im = mag * jnp.cos(z_im), mag * jnp.sin(z_im)
    den = lam_re * lam_re + lam_im * lam_im
    nr, ni = ab_re - 1.0, ab_im
    f_re = (nr * lam_re + ni * lam_im) / den
    f_im = (ni * lam_re - nr * lam_im) / den
    bb_re = f_re[..., None] * b_re - f_im[..., None] * b_im
    bb_im = f_re[..., None] * b_im + f_im[..., None] * b_re

    k = jnp.arange(_L + 1, dtype=_f32)[:, None, None, None]
    pw_mag = jnp.exp(k * z_re)
    pw_re, pw_im = pw_mag * jnp.cos(k * z_im), pw_mag * jnp.sin(k * z_im)

    abb_re = pw_re[..., None] * bb_re - pw_im[..., None] * bb_im
    abb_im = pw_re[..., None] * bb_im + pw_im[..., None] * bb_re

    lag = (jnp.einsum('kdgph,dgop->kdgho', abb_re[:_L], c_re, precision=hi)
           - jnp.einsum('kdgph,dgop->kdgho', abb_im[:_L], c_im, precision=hi))
    s_idx = np.arange(_L)[:, None]
    t_idx = np.arange(_L)[None, :]
    k_idx = np.arange(_L)[:, None, None]
    place = np.stack([t_idx - s_idx == k_idx, s_idx - t_idx == k_idx], axis=1).astype(np.float32)
    toep = jnp.einsum('kdst,kdgho->gshto', place, lag, precision=hi).reshape(_N_GROUPS, _GW, _GW)

    rev = np.arange(_L - 1, -1, -1)
    st = jnp.stack([abb_re[rev, 0], abb_im[rev, 0], abb_re[:_L, 1], abb_im[:_L, 1]], axis=0)
    w_state = st.transpose(2, 1, 4, 0, 3).reshape(_N_GROUPS, _GW, 4 * _STATE_DIM)

    def out_mat(d, powers):
        pr_, pi_ = pw_re[powers, d], pw_im[powers, d]
        cr, ci = c_re[d], c_im[d]
        wr = cr[None] * pr_[:, :, None, :] - ci[None] * pi_[:, :, None, :]
        wi = cr[None] * pi_[:, :, None, :] + ci[None] * pr_[:, :, None, :]
        return wr, -wi

    wfr, wfi = out_mat(0, np.arange(1, _L + 1))
    wbr, wbi = out_mat(1, np.arange(_L, 0, -1))
    wo = jnp.stack([wfr, wfi, wbr, wbi], axis=0)
    w_out = wo.transpose(2, 0, 4, 1, 3).reshape(_N_GROUPS, 4 * _STATE_DIM, _GW)

    flow = np.stack([np.arange(_SUBLANES), np.arange(_SUBLANES - 1, -1, -1)])[:, :, None, None]

    def chunk_pow(m):
        e = jnp.asarray(m, _f32) * float(_L)
        pm = jnp.exp(e * z_re[:, None])
        return jnp.stack([pm * jnp.cos(e * z_im[:, None]), pm * jnp.sin(e * z_im[:, None])], axis=1)

    def masked(d):
        return chunk_pow(np.full_like(flow, d)) * jnp.asarray(flow >= d, _f32)[:, None]

    scan_tab = jnp.stack([chunk_pow(np.ones_like(flow)), masked(1), masked(2), masked(4), chunk_pow(flow)], axis=1)
    scan_tab = scan_tab.reshape(2, 5, 2, _SUBLANES, _STATE_W)
    return toep.astype(_bf16), w_state.astype(_bf16), w_out.astype(_bf16), scan_tab


def _expand_kernel(a_ref, o_ref, *, row_run, col_run):
    n = a_ref.shape[1]
    wide = _SG * n
    j = lax.broadcasted_iota(jnp.int32, (n, wide), 0)
    c = lax.broadcasted_iota(jnp.int32, (n, wide), 1)
    shift = col_run.bit_length() - 1
    spread = lax.shift_left(lax.shift_right_logical(j, shift), shift + _SG.bit_length() - 1) + (j & (col_run - 1))
    for g in range(_SG):
        onehot = jnp.where(c == spread + g * col_run, 1.0, 0.0).astype(_bf16)
        xg = _dot(a_ref[g], onehot).astype(_bf16)
        for b in range(n // row_run):
            r0 = (b * _SG + g) * row_run
            o_ref[r0:r0 + row_run, :] = xg[b * row_run:(b + 1) * row_run, :]


def _expand(per_group, row_run, col_run):
    n = per_group.shape[1]
    return pl.pallas_call(
        functools.partial(_expand_kernel, row_run=row_run, col_run=col_run),
        grid=(_N_SLABS,),
        in_specs=[pl.BlockSpec((_SG, n, n), lambda q: (q, 0, 0))],
        out_specs=pl.BlockSpec((None, _SG * n, _SG * n), lambda q: (q, 0, 0)),
        out_shape=jax.ShapeDtypeStruct((_N_SLABS, _SG * n, _SG * n), _bf16),
        compiler_params=_params(("arbitrary",)),
        name="s5_expand",
    )(per_group)


def _s5_slab_tables(*s5_params):
    toep, w_state, w_out, scan_tab = _s5_tables(*s5_params)
    return (_expand(toep, _GROUP_DIM, _GROUP_DIM), _expand(w_state, _GROUP_DIM, _STATE_DIM),
            _expand(w_out, _STATE_DIM, _GROUP_DIM), scan_tab)


def _chunk_tokens(x_ref, sc_ref, sh_ref):
    rows = x_ref.shape[0] // _L
    sc1 = 1.0 + sc_ref[...]
    sh = sh_ref[...]
    return [x_ref[pl.ds(s, rows, stride=_L), :] * sc1 + sh for s in range(_L)]


def _low_half(rows):
    return lax.broadcasted_iota(jnp.int32, (rows, _LANES), 1) < _SLAB_LANES


def _slab_chunks(us):
    low = _low_half(us[0].shape[0])
    lo, hi = [], []
    for m in range(_L // 2):
        a, b = us[2 * m], us[2 * m + 1]
        lo.append(jnp.where(low, a, pltpu.roll(b, _SLAB_LANES, axis=1)))
        hi.append(jnp.where(low, pltpu.roll(a, _SLAB_LANES, axis=1), b))
    return (jnp.concatenate(lo, axis=-1).astype(_bf16), jnp.concatenate(hi, axis=-1).astype(_bf16))


def _s5_state_kernel(x_ref, sc_ref, sh_ref, w_ref, *o_refs):
    us = _chunk_tokens(x_ref, sc_ref, sh_ref)
    for h, u in enumerate(_slab_chunks(us)):
        r = _dot(u, w_ref[h])
        for c, o_ref in enumerate(o_refs):
            o_ref[:, h * _CS:(h + 1) * _CS] = r[:, c * _CS:(c + 1) * _CS]


def _s5_scan_kernel(lfr, lfi, lbr, lbi, tab_ref, ofr, ofi, obr, obi, carry):
    @pl.when(pl.program_id(2) == 0)
    def _():
        carry[...] = jnp.zeros_like(carry)

    tiles = lfr.shape[0] // _SUBLANES
    sub = lax.broadcasted_iota(jnp.int32, (_SUBLANES, lfr.shape[1]), 0)

    def cmul(ar, ai, br, bi):
        return ar * br - ai * bi, ar * bi + ai * br

    def tile_scan(xr, xi, f0r, f0i, d):
        def shift(v, k):
            return pltpu.roll(v, k if d == 0 else _SUBLANES - k, axis=0)

        ir, ii = xr, xi
        for n, k in ((1, 1), (2, 2), (3, 4)):
            pr, pi = cmul(tab_ref[d, n, 0], tab_ref[d, n, 1], shift(ir, k), shift(ii, k))
            ir, ii = ir + pr, ii + pi
        first = sub == (0 if d == 0 else _SUBLANES - 1)
        cr, ci = cmul(tab_ref[d, 4, 0], tab_ref[d, 4, 1], f0r, f0i)
        fin_r = cr + jnp.where(first, 0.0, shift(ir, 1))
        fin_i = ci + jnp.where(first, 0.0, shift(ii, 1))
        nr, ni = cmul(tab_ref[d, 0, 0], tab_ref[d, 0, 1], fin_r, fin_i)
        nr, ni = nr + xr, ni + xi
        end = _SUBLANES - 1 if d == 0 else 0
        return (fin_r, fin_i, jnp.broadcast_to(nr[end:end + 1], nr.shape),
                jnp.broadcast_to(ni[end:end + 1], ni.shape))

    def body(i, c):
        rf = pl.multiple_of(i * _SUBLANES, _SUBLANES)
        rb = pl.multiple_of((tiles - 1 - i) * _SUBLANES, _SUBLANES)
        fr, fi, c0, c1 = tile_scan(lfr[pl.ds(rf, _SUBLANES), :], lfi[pl.ds(rf, _SUBLANES), :], c[0], c[1], 0)
        ofr[pl.ds(rf, _SUBLANES), :] = fr
        ofi[pl.ds(rf, _SUBLANES), :] = fi
        br, bi, c2, c3 = tile_scan(lbr[pl.ds(rb, _SUBLANES), :], lbi[pl.ds(rb, _SUBLANES), :], c[2], c[3], 1)
        obr[pl.ds(rb, _SUBLANES), :] = br
        obi[pl.ds(rb, _SUBLANES), :] = bi
        return (c0, c1, c2, c3)

    c = lax.fori_loop(0, tiles, body, (carry[0], carry[1], carry[2], carry[3]))
    for q in range(4):
        carry[q] = c[q]


def _s5_out_kernel(x_ref, sc_ref, sh_ref, d_ref, sfr, sfi, sbr, sbi, t_ref, w_ref, o_ref):
    s_refs = (sfr, sfi, sbr, sbi)
    rows = x_ref.shape[0] // _L
    d = d_ref[...]
    us = _chunk_tokens(x_ref, sc_ref, sh_ref)
    ys = []
    for h, u in enumerate(_slab_chunks(us)):
        st = jnp.concatenate([s[:, h * _CS:(h + 1) * _CS] for s in s_refs], axis=-1).astype(_bf16)
        ys.append(_dot(u, t_ref[h]) + _dot(st, w_ref[h]))
    low = _low_half(rows)
    for t in range(_L):
        cols = slice((t // 2) * _LANES, (t // 2 + 1) * _LANES)
        y_lo, y_hi = ys[0][:, cols], ys[1][:, cols]
        if t % 2 == 0:
            y_t = jnp.where(low, y_lo, pltpu.roll(y_hi, _SLAB_LANES, axis=1))
        else:
            y_t = jnp.where(low, pltpu.roll(y_lo, _SLAB_LANES, axis=1), y_hi)
        o_ref[pl.ds(t, rows, stride=_L), :] = jax.nn.gelu(d * us[t] + y_t)


def _glu_ln_kernel(y_ref, x_ref, w_ref, b_ref, gate_ref, lng_ref, lnb_ref, o_ref):
    half = y_ref.shape[0] // 2
    for r in (slice(0, half), slice(half, 2 * half)):
        ag = _dot(y_ref[r, :].astype(_bf16), w_ref[...]) + b_ref[...]
        mix = ag[:, :_D] * jax.nn.sigmoid(ag[:, _D:])
        z = _ALPHA * x_ref[r, :] + gate_ref[...] * mix
        o_ref[r, :] = _layer_norm(z, lng_ref[...], lnb_ref[...])


def _s5_layer(x, sc, sh, gate, tables, d_skip, w_glu, b_glu, ln_g, ln_b):
    bsz, slen, _ = x.shape
    cs = slen // _L
    nt = bsz * slen
    toep, w_state, w_out, scan_tab = tables
    rows = min(cs, _S5_ROWS)
    bpc = cs // rows
    nblk = bsz * bpc
    x2 = x.reshape(nt, _D)
    sc3, sh3 = sc.reshape(bsz, 1, _D), sh.reshape(bsz, 1, _D)

    x_spec = pl.BlockSpec((rows * _L, _LANES), lambda q, i: (i, q))
    mod_spec = pl.BlockSpec((None, 1, _LANES), lambda q, i: (i // bpc, 0, q))
    st_spec = pl.BlockSpec((None, rows, _TS), lambda q, i: (i // bpc, i % bpc, q))
    st_shape = jax.ShapeDtypeStruct((bsz, cs, _STATE_W), _f32)
    slab_spec = pl.BlockSpec((2, _CW, _CW), lambda q, i: (q, 0, 0))

    loc = pl.pallas_call(
        _s5_state_kernel,
        grid=(_N_TILES, nblk),
        in_specs=[x_spec, mod_spec, mod_spec, slab_spec],
        out_specs=[st_spec] * 4,
        out_shape=[st_shape] * 4,
        compiler_params=_params(("arbitrary", "arbitrary")),
        name="s5_chunk_states",
    )(x2, sc3, sh3, w_state)

    cblk = min(cs, _SCAN_ROWS)
    nsb = cs // cblk
    fwd_spec = pl.BlockSpec((None, cblk, _TS), lambda b, q, j: (b, j, q))
    bwd_spec = pl.BlockSpec((None, cblk, _TS), lambda b, q, j: (b, nsb - 1 - j, q))
    sin = pl.pallas_call(
        _s5_scan_kernel,
        grid=(bsz, _N_TILES, nsb),
        in_specs=[fwd_spec, fwd_spec, bwd_spec, bwd_spec,
                  pl.BlockSpec((2, 5, 2, _SUBLANES, _TS), lambda b, q, j: (0, 0, 0, 0, q))],
        out_specs=[fwd_spec, fwd_spec, bwd_spec, bwd_spec],
        out_shape=[st_shape] * 4,
        scratch_shapes=[pltpu.VMEM((4, _SUBLANES, _TS), _f32)],
        compiler_params=_params(("arbitrary", "arbitrary", "arbitrary")),
        name="s5_chunk_scan",
    )(*loc, scan_tab)

    y = pl.pallas_call(
        _s5_out_kernel,
        grid=(_N_TILES, nblk),
        in_specs=[x_spec, mod_spec, mod_spec, pl.BlockSpec((1, _LANES), lambda q, i: (0, q)),
                  st_spec, st_spec, st_spec, st_spec, slab_spec, slab_spec],
        out_specs=x_spec,
        out_shape=jax.ShapeDtypeStruct((nt, _D), _f32),
        compiler_params=_params(("arbitrary", "arbitrary")),
        name="s5_chunk_out",
    )(x2, sc3, sh3, d_skip.reshape(1, _D), *sin, toep, w_out)

    bps = slen // _TM
    row_spec = pl.BlockSpec((_TM, _D), lambda i: (i, 0))
    seq_spec = pl.BlockSpec((None, 1, _D), lambda i: (i // bps, 0, 0))
    out = pl.pallas_call(
        _glu_ln_kernel,
        grid=(nt // _TM,),
        in_specs=[row_spec, row_spec, _const_spec((_D, 2 * _D)), _const_spec((1, 2 * _D)),
                  seq_spec, _const_spec((1, _D)), _const_spec((1, _D))],
        out_specs=row_spec,
        out_shape=jax.ShapeDtypeStruct((nt, _D), _f32),
        compiler_params=_params(("arbitrary",)),
        name="s5_glu_ln",
    )(y, x2, w_glu, b_glu.reshape(1, 2 * _D), gate.reshape(bsz, 1, _D),
      ln_g.reshape(1, _D), ln_b.reshape(1, _D))
    return out.reshape(bsz, slen, _D)


def _ffn_kernel(xp_ref, x_ref, xn_ref, sc_ref, sh_ref, gate_ref, win_ref, bin_ref, cw_ref, cb_ref,
                wout_ref, bout_ref, lng_ref, lnb_ref, o_ref, *, blocks_per_seq):
    i = pl.program_id(0)
    first = (i % blocks_per_seq) == 0
    last = (i % blocks_per_seq) == blocks_per_seq - 1
    halo = _BF16_ROWS
    rows = _TM + 2 * halo

    sc1 = 1.0 + sc_ref[...]
    sh = sh_ref[...]
    x = x_ref[...]
    h = (x * sc1 + sh).astype(_bf16)
    h_prev = (xp_ref[...] * sc1 + sh).astype(_bf16)
    h_next = (xn_ref[...] * sc1 + sh).astype(_bf16)
    h_ext = jnp.concatenate([h_prev, h, h_next], axis=0)

    rid = lax.broadcasted_iota(jnp.int32, (rows, 1), 0)
    pad = (first & (rid == halo - 1)) | (last & (rid == halo + _TM))

    y = jnp.zeros((_TM, _D), _f32)
    for lo, hi in zip(_FFN_SPLIT[:-1], _FFN_SPLIT[1:]):
        u = _dot(h_ext, win_ref[:, lo:hi]) + bin_ref[:, lo:hi]
        u = jnp.where(pad, 0.0, u)
        g = _dot(h, win_ref[:, _D_FF + lo:_D_FF + hi]) + bin_ref[:, _D_FF + lo:_D_FF + hi]
        up = pltpu.roll(u, 1, axis=0)[halo:halo + _TM]
        dn = pltpu.roll(u, rows - 1, axis=0)[halo:halo + _TM]
        conv = (up * cw_ref[0:1, lo:hi] + u[halo:halo + _TM] * cw_ref[1:2, lo:hi]
                + dn * cw_ref[2:3, lo:hi] + cb_ref[:, lo:hi])
        act = (jax.nn.gelu(conv) * g).astype(_bf16)
        y = y + _dot(act, wout_ref[lo:hi, :])
    y = y + bout_ref[...]
    z = _ALPHA * x + gate_ref[...] * y
    o_ref[...] = _layer_norm(z, lng_ref[...], lnb_ref[...])


def _ffn_layer(x, sc, sh, gate, w_in, b_in, conv_w, conv_b, w_out, b_out, ln_g, ln_b):
    bsz, slen, _ = x.shape
    nt = bsz * slen
    bps = slen // _TM
    hb = _TM // _BF16_ROWS
    n_halo = nt // _BF16_ROWS
    row_spec = pl.BlockSpec((_TM, _D), lambda i: (i, 0))
    prev_spec = pl.BlockSpec((_BF16_ROWS, _D), lambda i: (jnp.maximum(i * hb - 1, 0), 0))
    next_spec = pl.BlockSpec((_BF16_ROWS, _D), lambda i: (jnp.minimum((i + 1) * hb, n_halo - 1), 0))
    seq_spec = pl.BlockSpec((None, 1, _D), lambda i: (i // bps, 0, 0))
    x2 = x.reshape(nt, _D)
    out = pl.pallas_call(
        functools.partial(_ffn_kernel, blocks_per_seq=bps),
        grid=(nt // _TM,),
        in_specs=[prev_spec, row_spec, next_spec, seq_spec, seq_spec, seq_spec,
                  _const_spec((_D, 2 * _D_FF)), _const_spec((1, 2 * _D_FF)),
                  _const_spec((3, _D_FF)), _const_spec((1, _D_FF)),
                  _const_spec((_D_FF, _D)), _const_spec((1, _D)),
                  _const_spec((1, _D)), _const_spec((1, _D))],
        out_specs=row_spec,
        out_shape=jax.ShapeDtypeStruct((nt, _D), _f32),
        compiler_params=_params(("arbitrary",)),
        name="conv_ffn_ln",
    )(x2, x2, x2, sc.reshape(bsz, 1, _D), sh.reshape(bsz, 1, _D), gate.reshape(bsz, 1, _D),
      w_in, b_in.reshape(1, 2 * _D_FF), conv_w, conv_b.reshape(1, _D_FF),
      w_out, b_out.reshape(1, _D), ln_g.reshape(1, _D), ln_b.reshape(1, _D))
    return out.reshape(bsz, slen, _D)


def _qkv_kernel(x_ref, sc_ref, sh_ref, w_ref, b_ref, q_ref, k_ref, v_ref):
    h = (x_ref[...] * (1.0 + sc_ref[...]) + sh_ref[...]).astype(_bf16)
    qkv = _dot(h, w_ref[...]) + b_ref[...]
    q_ref[...] = (qkv[:, :_D] * (_HEAD_DIM ** -0.5)).astype(_bf16)
    k_ref[...] = qkv[:, _D:2 * _D].astype(_bf16)
    v_ref[...] = qkv[:, 2 * _D:].astype(_bf16)


def _bias_table(rpb):
    col = np.arange(_GRID_W)
    start = np.clip(col - _WIN_W // 2, 0, _GRID_W - _WIN_W)
    kc = col[None, :]
    inside = (kc >= start[:, None]) & (kc < start[:, None] + _WIN_W)
    off = np.clip(kc - col[:, None] + (_WIN_W - 1), 0, 2 * _WIN_W - 2)
    t2 = jnp.where(jnp.asarray(inside)[None, None], rpb[:, :, off], _NEG)
    return jnp.concatenate([t2[:, :-1], t2[:, 1:]], axis=-1)


def _attn_kernel(q_ref, kp_ref, kc_ref, kn_ref, vp_ref, vc_ref, vn_ref, t_ref, x_ref, gate_ref,
                 wo_ref, bo_ref, lng_ref, lnb_ref, o_ref, kcat, vcat, att, *, grid_rows):
    jb = pl.program_id(1)
    blk = _Q_ROWS * _GRID_W
    for n, (kr, vr) in enumerate(((kp_ref, vp_ref), (kc_ref, vc_ref), (kn_ref, vn_ref))):
        kcat[n * blk:(n + 1) * blk, :] = kr[...]
        vcat[n * blk:(n + 1) * blk, :] = vr[...]

    lane = lax.broadcasted_iota(jnp.int32, (_GRID_W, 2 * _HEAD_DIM), 1)
    low = lane < _HEAD_DIM
    n_pairs = _N_HEADS // 2

    def row_body(i, carry):
        r = _Q_ROWS * jb + i
        rs = jnp.clip(r - _WIN_H // 2, 0, grid_rows - _WIN_H)
        off = pl.multiple_of((rs - _Q_ROWS * jb + _Q_ROWS) * _GRID_W, _GRID_W)
        ro0 = rs - r + (_WIN_H - 1)
        qrow = pl.multiple_of(i * _GRID_W, _GRID_W)
        cols = [slice(hp * 2 * _HEAD_DIM, (hp + 1) * 2 * _HEAD_DIM) for hp in range(n_pairs)]
        scores = []
        for hp in range(n_pairs):
            q2 = q_ref[pl.ds(qrow, _GRID_W), cols[hp]]
            k2 = kcat[pl.ds(off, _BAND), cols[hp]]
            for e in range(2):
                qe = jnp.where(low if e == 0 else ~low, q2, jnp.zeros_like(q2))
                scores.append(lax.dot_general(qe, k2, (((1,), (1,)), ((), ())), preferred_element_type=_f32))
        probs, denoms = [], []
        for h, s in enumerate(scores):
            bias = jnp.concatenate([t_ref[h, ro0 + 2 * m] for m in range(_WIN_H // 2)], axis=-1)
            s = s + bias
            p = jnp.exp(s - jnp.max(s, axis=-1, keepdims=True))
            denoms.append(jnp.sum(p, axis=-1, keepdims=True))
            probs.append(p.astype(_bf16))
        for hp in range(n_pairs):
            v2 = vcat[pl.ds(off, _BAND), cols[hp]]
            o0 = _dot(probs[2 * hp], v2) / denoms[2 * hp]
            o1 = _dot(probs[2 * hp + 1], v2) / denoms[2 * hp + 1]
            att[pl.ds(qrow, _GRID_W), cols[hp]] = jnp.where(low, o0, o1)
        return carry

    lax.fori_loop(0, _Q_ROWS, row_body, 0)

    y = _dot(att[...].astype(_bf16), wo_ref[...]) + bo_ref[...]
    z = _ALPHA * x_ref[...] + gate_ref[...] * y
    o_ref[...] = _layer_norm(z, lng_ref[...], lnb_ref[...])


def _attn_layer(x, sc, sh, gate, w_qkv, b_qkv, rpb, w_o, b_o, ln_g, ln_b):
    bsz, slen, _ = x.shape
    nt = bsz * slen
    bps = slen // _TM
    x2 = x.reshape(nt, _D)
    row_spec = pl.BlockSpec((_TM, _D), lambda i: (i, 0))
    seq_spec = pl.BlockSpec((None, 1, _D), lambda i: (i // bps, 0, 0))
    qkv_shape = jax.ShapeDtypeStruct((nt, _D), _bf16)
    q, k, v = pl.pallas_call(
        _qkv_kernel,
        grid=(nt // _TM,),
        in_specs=[row_spec, seq_spec, seq_spec, _const_spec((_D, 3 * _D)), _const_spec((1, 3 * _D))],
        out_specs=[row_spec] * 3,
        out_shape=[qkv_shape] * 3,
        compiler_params=_params(("arbitrary",)),
        name="attn_qkv",
    )(x2, sc.reshape(bsz, 1, _D), sh.reshape(bsz, 1, _D), w_qkv, b_qkv.reshape(1, 3 * _D))

    grid_rows = slen // _GRID_W
    nqb = grid_rows // _Q_ROWS
    blk = _Q_ROWS * _GRID_W
    cur = pl.BlockSpec((blk, _D), lambda b, j: (b * nqb + j, 0))
    prev = pl.BlockSpec((blk, _D), lambda b, j: (b * nqb + jnp.maximum(j - 1, 0), 0))
    nxt = pl.BlockSpec((blk, _D), lambda b, j: (b * nqb + jnp.minimum(j + 1, nqb - 1), 0))
    bseq = pl.BlockSpec((None, 1, _D), lambda b, j: (b, 0, 0))
    table = _bias_table(rpb)
    out = pl.pallas_call(
        functools.partial(_attn_kernel, grid_rows=grid_rows),
        grid=(bsz, nqb),
        in_specs=[cur, prev, cur, nxt, prev, cur, nxt, _const_spec(table.shape), cur, bseq,
                  _const_spec((_D, _D)), _const_spec((1, _D)), _const_spec((1, _D)), _const_spec((1, _D))],
        out_specs=cur,
        out_shape=jax.ShapeDtypeStruct((nt, _D), _f32),
        scratch_shapes=[pltpu.VMEM((3 * blk, _D), _bf16), pltpu.VMEM((3 * blk, _D), _bf16),
                        pltpu.VMEM((blk, _D), _f32)],
        compiler_params=_params(("arbitrary", "arbitrary")),
        name="nbr_attn_ln",
    )(q, k, k, k, v, v, v, table, x2, gate.reshape(bsz, 1, _D), w_o, b_o.reshape(1, _D),
      ln_g.reshape(1, _D), ln_b.reshape(1, _D))
    return out.reshape(bsz, slen, _D)


def _trunk(x, mod, s5_tables, s5_d, w_glu, b_glu, w_qkv, b_qkv, rpb, w_o, b_o,
           ffn_w_in, ffn_b_in, ffn_conv_w, ffn_conv_b, ffn_w_out, ffn_b_out, ln_g, ln_b):
    for i in range(_DEPTH):
        sh_m, sc_m, g_m, sh_f, sc_f, g_f = [mod[i, :, n * _D:(n + 1) * _D] for n in range(6)]
        if i % 2 == 0:
            x = _s5_layer(x, sc_m, sh_m, g_m, s5_tables, s5_d, w_glu, b_glu, ln_g[i, 0], ln_b[i, 0])
        else:
            x = _attn_layer(x, sc_m, sh_m, g_m, w_qkv, b_qkv, rpb, w_o, b_o, ln_g[i, 0], ln_b[i, 0])
        x = _ffn_layer(x, sc_f, sh_f, g_f, ffn_w_in[i], ffn_b_in[i], ffn_conv_w[i], ffn_conv_b[i],
                       ffn_w_out[i], ffn_b_out[i], ln_g[i, 1], ln_b[i, 1])
    return x


def kernel(x_prompt, x_sample, c_prompt, c_sample, w_ada, b_ada, ln_g, ln_b, s5_lam_re, s5_lam_im, s5_log_dt, s5_b_re, s5_b_im, s5_c_re, s5_c_im, s5_d, s5_w_glu, s5_b_glu, na_w_qkv, na_b_qkv, na_rpb, na_w_o, na_b_o, ffn_w_in, ffn_b_in, ffn_conv_w, ffn_conv_b, ffn_w_out, ffn_b_out):
    assert _DEPTH == 2 and x_prompt.shape[-1] == _D
    nb_p, nb_s = c_prompt.shape[0], c_sample.shape[0]
    nb_pad = -(-(nb_p + nb_s) // _SUBLANES) * _SUBLANES
    c_all = jnp.zeros((nb_pad, _D), _f32).at[:nb_p].set(c_prompt).at[nb_p:nb_p + nb_s].set(c_sample)
    mod = _modulation(c_all, w_ada, b_ada)

    tables = _s5_slab_tables(s5_lam_re[0], s5_lam_im[0], s5_log_dt[0], s5_b_re[0], s5_b_im[0],
                        s5_c_re[0], s5_c_im[0])
    shared = (tables, s5_d[0], s5_w_glu[0].astype(_bf16), s5_b_glu[0],
              na_w_qkv[0].astype(_bf16), na_b_qkv[0], na_rpb[0], na_w_o[0].astype(_bf16), na_b_o[0],
              ffn_w_in.astype(_bf16), ffn_b_in, ffn_conv_w, ffn_conv_b, ffn_w_out.astype(_bf16), ffn_b_out,
              ln_g, ln_b)
    y_prompt = _trunk(x_prompt, mod[:, :nb_p], *shared)
    y_sample = _trunk(x_sample, mod[:, nb_p:nb_p + nb_s], *shared)
    return (y_prompt, y_sample)
```

```python
import functools

import numpy as np
import jax
import jax.numpy as jnp
from jax import lax
from jax.experimental import pallas as pl
from jax.experimental.pallas import tpu as pltpu

_D = 1024
_DEPTH = 2
_GRID_W = 64
_GROUP_DIM = 16
_N_GROUPS = _D // _GROUP_DIM
_STATE_DIM = 64
_N_HEADS = 16
_HEAD_DIM = _D // _N_HEADS
_WIN_H = 8
_WIN_W = 16
_D_FF = 2816
_ALPHA = (2 * _DEPTH) ** 0.25
_LN_EPS = 1e-5

_LANES = 128
_SUBLANES = 8
_BF16_ROWS = 16
_VMEM_LIMIT = 56 * 1024 * 1024

_L = 16
_GW = _L * _GROUP_DIM
_SG = 4
_SLAB_LANES = _SG * _GROUP_DIM
_N_SLABS = _N_GROUPS // _SG
_N_TILES = _D // _LANES
_CW = _L * _SLAB_LANES
_CS = _SG * _STATE_DIM
_TS = 2 * _CS
_STATE_W = _N_GROUPS * _STATE_DIM
_S5_ROWS = 256
_S5_PARTS = 1
_SCAN_ROWS = 256

_TM = 512
_FFN_TM = 1024
_FFN_SPLIT = (0, 1024, 2048, _D_FF)
_Q_ROWS = 8
_BAND = _WIN_H * _GRID_W
_ATT_SKEW = 8
_NEG = -1e30

_f32 = jnp.float32
_bf16 = jnp.bfloat16


def _params(sem):
    return pltpu.CompilerParams(dimension_semantics=sem, vmem_limit_bytes=_VMEM_LIMIT)


def _const_spec(shape):
    nd = len(shape)
    return pl.BlockSpec(shape, lambda *_: (0,) * nd, pipeline_mode=pl.Buffered(1))


def _layer_norm(z, g, b):
    mu = jnp.mean(z, axis=-1, keepdims=True)
    d = z - mu
    var = jnp.mean(d * d, axis=-1, keepdims=True)
    return d * lax.rsqrt(var + _LN_EPS) * g + b


def _dot(a, b):
    return jnp.dot(a, b, preferred_element_type=_f32)


def _mod_kernel(c_ref, w_ref, b_ref, o_ref):
    cond = jax.nn.silu(c_ref[...]).astype(_bf16)
    o_ref[...] = _dot(cond, w_ref[...].astype(_bf16)) + b_ref[...]


def _modulation(c_all, w_ada, b_ada):
    nb = c_all.shape[0]
    tn = 1536
    return pl.pallas_call(
        _mod_kernel,
        grid=(_DEPTH, 6 * _D // tn),
        in_specs=[
            pl.BlockSpec((nb, _D), lambda l, n: (0, 0)),
            pl.BlockSpec((None, _D, tn), lambda l, n: (l, 0, n)),
            pl.BlockSpec((None, 1, tn), lambda l, n: (l, 0, n)),
        ],
        out_specs=pl.BlockSpec((None, nb, tn), lambda l, n: (l, 0, n)),
        out_shape=jax.ShapeDtypeStruct((_DEPTH, nb, 6 * _D), _f32),
        compiler_params=_params(("arbitrary", "arbitrary")),
        name="adaln_mod",
    )(c_all, w_ada, b_ada.reshape(_DEPTH, 1, 6 * _D))


def _s5_tables(lam_re, lam_im, log_dt, b_re, b_im, c_re, c_im):
    hi = lax.Precision.HIGHEST
    dt = jnp.exp(log_dt)[..., None]
    z_re, z_im = lam_re * dt, lam_im * dt
    mag = jnp.exp(z_re)
    ab_re, ab_im = mag * jnp.cos(z_im), mag * jnp.sin(z_im)
    den = lam_re * lam_re + lam_im * lam_im
    nr, ni = ab_re - 1.0, ab_im
    f_re = (nr * lam_re + ni * lam_im) / den
    f_im = (ni * lam_re - nr * lam_im) / den
    bb_re = f_re[..., None] * b_re - f_im[..., None] * b_im
    bb_im = f_re[..., None] * b_im + f_im[..., None] * b_re

    k = jnp.arange(_L + 1, dtype=_f32)[:, None, None, None]
    pw_mag = jnp.exp(k * z_re)
    pw_re, pw_im = pw_mag * jnp.cos(k * z_im), pw_mag * jnp.sin(k * z_im)

    abb_re = pw_re[..., None] * bb_re - pw_im[..., None] * bb_im
    abb_im = pw_re[..., None] * bb_im + pw_im[..., None] * bb_re

    lag = (jnp.einsum('kdgph,dgop->kdgho', abb_re[:_L], c_re, precision=hi)
           - jnp.einsum('kdgph,dgop->kdgho', abb_im[:_L], c_im, precision=hi))
    s_idx = np.arange(_L)[:, None]
    t_idx = np.arange(_L)[None, :]
    k_idx = np.arange(_L)[:, None, None]
    place = np.stack([t_idx - s_idx == k_idx, s_idx - t_idx == k_idx], axis=1).astype(np.float32)
    toep = jnp.einsum('kdst,kdgho->gshto', place, lag, precision=hi).reshape(_N_GROUPS, _GW, _GW)

    rev = np.arange(_L - 1, -1, -1)
    st = jnp.stack([abb_re[rev, 0], abb_im[rev, 0], abb_re[:_L, 1], abb_im[:_L, 1]], axis=0)
    w_state = st.transpose(2, 1, 4, 0, 3).reshape(_N_GROUPS, _GW, 4 * _STATE_DIM)

    def out_mat(d, powers):
        pr_, pi_ = pw_re[powers, d], pw_im[powers, d]
        cr, ci = c_re[d], c_im[d]
        wr = cr[None] * pr_[:, :, None, :] - ci[None] * pi_[:, :, None, :]
        wi = cr[None] * pi_[:, :, None, :] + ci[None] * pr_[:, :, None, :]
        return wr, -wi

    wfr, wfi = out_mat(0, np.arange(1, _L + 1))
    wbr, wbi = out_mat(1, np.arange(_L, 0, -1))
    wo = jnp.stack([wfr, wfi, wbr, wbi], axis=0)
    w_out = wo.transpose(2, 0, 4, 1, 3).reshape(_N_GROUPS, 4 * _STATE_DIM, _GW)

    flow = np.stack([np.arange(_SUBLANES), np.arange(_SUBLANES - 1, -1, -1)])[:, :, None, None]

    def chunk_pow(m):
        e = jnp.asarray(m, _f32) * float(_L)
        pm = jnp.exp(e * z_re[:, None])
        return jnp.stack([pm * jnp.cos(e * z_im[:, None]), pm * jnp.sin(e * z_im[:, None])], axis=1)

    def masked(d):
        return chunk_pow(np.full_like(flow, d)) * jnp.asarray(flow >= d, _f32)[:, None]

    scan_tab = jnp.stack([chunk_pow(np.ones_like(flow)), masked(1), masked(2), masked(4), chunk_pow(flow)], axis=1)
    scan_tab = scan_tab.reshape(2, 5, 2, _SUBLANES, _STATE_W)
    return toep.astype(_bf16), w_state.astype(_bf16), w_out.astype(_bf16), scan_tab


def _expand_kernel(a_ref, o_ref, *, row_run, col_run):
    n = a_ref.shape[1]
    wide = _SG * n
    j = lax.broadcasted_iota(jnp.int32, (n, wide), 0)
    c = lax.broadcasted_iota(jnp.int32, (n, wide), 1)
    shift = col_run.bit_length() - 1
    spread = lax.shift_left(lax.shift_right_logical(j, shift), shift + _SG.bit_length() - 1) + (j & (col_run - 1))
    for g in range(_SG):
        onehot = jnp.where(c == spread + g * col_run, 1.0, 0.0).astype(_bf16)
        xg = _dot(a_ref[g], onehot).astype(_bf16)
        for b in range(n // row_run):
            r0 = (b * _SG + g) * row_run
            o_ref[r0:r0 + row_run, :] = xg[b * row_run:(b + 1) * row_run, :]


def _expand(per_group, row_run, col_run):
    n = per_group.shape[1]
    return pl.pallas_call(
        functools.partial(_expand_kernel, row_run=row_run, col_run=col_run),
        grid=(_N_SLABS,),
        in_specs=[pl.BlockSpec((_SG, n, n), lambda q: (q, 0, 0))],
        out_specs=pl.BlockSpec((None, _SG * n, _SG * n), lambda q: (q, 0, 0)),
        out_shape=jax.ShapeDtypeStruct((_N_SLABS, _SG * n, _SG * n), _bf16),
        compiler_params=_params(("arbitrary",)),
        name="s5_expand",
    )(per_group)


def _s5_slab_tables(*s5_params):
    toep, w_state, w_out, scan_tab = _s5_tables(*s5_params)
    return (_expand(toep, _GROUP_DIM, _GROUP_DIM), _expand(w_state, _GROUP_DIM, _STATE_DIM),
            _expand(w_out, _STATE_DIM, _GROUP_DIM), scan_tab)


def _chunk_tokens(x_ref, sc_ref, sh_ref, c0, rows):
    sc1 = 1.0 + sc_ref[...]
    sh = sh_ref[...]
    return [x_ref[pl.ds(c0 * _L + s, rows, stride=_L), :] * sc1 + sh for s in range(_L)]


def _low_half(rows):
    return lax.broadcasted_iota(jnp.int32, (rows, _LANES), 1) < _SLAB_LANES


def _slab_chunks(us):
    low = _low_half(us[0].shape[0])
    lo, hi = [], []
    for m in range(_L // 2):
        a, b = us[2 * m], us[2 * m + 1]
        lo.append(jnp.where(low, a, pltpu.roll(b, _SLAB_LANES, axis=1)))
        hi.append(jnp.where(low, pltpu.roll(a, _SLAB_LANES, axis=1), b))
    return (jnp.concatenate(lo, axis=-1).astype(_bf16), jnp.concatenate(hi, axis=-1).astype(_bf16))


def _row_parts(x_ref):
    rows = x_ref.shape[0] // _L
    part = rows // _S5_PARTS
    return [(p * part, part) for p in range(_S5_PARTS)]


def _s5_state_kernel(x_ref, sc_ref, sh_ref, w_ref, *o_refs):
    for c0, rows in _row_parts(x_ref):
        us = _chunk_tokens(x_ref, sc_ref, sh_ref, c0, rows)
        for h, u in enumerate(_slab_chunks(us)):
            r = _dot(u, w_ref[h])
            for c, o_ref in enumerate(o_refs):
                o_ref[c0:c0 + rows, h * _CS:(h + 1) * _CS] = r[:, c * _CS:(c + 1) * _CS]


def _s5_scan_kernel(lfr, lfi, lbr, lbi, tab_ref, ofr, ofi, obr, obi, carry):
    @pl.when(pl.program_id(2) == 0)
    def _():
        carry[...] = jnp.zeros_like(carry)

    tiles = lfr.shape[0] // _SUBLANES
    sub = lax.broadcasted_iota(jnp.int32, (_SUBLANES, lfr.shape[1]), 0)

    def cmul(ar, ai, br, bi):
        return ar * br - ai * bi, ar * bi + ai * br

    def tile_scan(xr, xi, f0r, f0i, d):
        def shift(v, k):
            return pltpu.roll(v, k if d == 0 else _SUBLANES - k, axis=0)

        ir, ii = xr, xi
        for n, k in ((1, 1), (2, 2), (3, 4)):
            pr, pi = cmul(tab_ref[d, n, 0], tab_ref[d, n, 1], shift(ir, k), shift(ii, k))
            ir, ii = ir + pr, ii + pi
        first = sub == (0 if d == 0 else _SUBLANES - 1)
        cr, ci = cmul(tab_ref[d, 4, 0], tab_ref[d, 4, 1], f0r, f0i)
        fin_r = cr + jnp.where(first, 0.0, shift(ir, 1))
        fin_i = ci + jnp.where(first, 0.0, shift(ii, 1))
        nr, ni = cmul(tab_ref[d, 0, 0], tab_ref[d, 0, 1], fin_r, fin_i)
        nr, ni = nr + xr, ni + xi
        end = _SUBLANES - 1 if d == 0 else 0
        return (fin_r, fin_i, jnp.broadcast_to(nr[end:end + 1], nr.shape),
                jnp.broadcast_to(ni[end:end + 1], ni.shape))

    def body(i, c):
        rf = pl.multiple_of(i * _SUBLANES, _SUBLANES)
        rb = pl.multiple_of((tiles - 1 - i) * _SUBLANES, _SUBLANES)
        fr, fi, c0, c1 = tile_scan(lfr[pl.ds(rf, _SUBLANES), :], lfi[pl.ds(rf, _SUBLANES), :], c[0], c[1], 0)
        ofr[pl.ds(rf, _SUBLANES), :] = fr
        ofi[pl.ds(rf, _SUBLANES), :] = fi
        br, bi, c2, c3 = tile_scan(lbr[pl.ds(rb, _SUBLANES), :], lbi[pl.ds(rb, _SUBLANES), :], c[2], c[3], 1)
        obr[pl.ds(rb, _SUBLANES), :] = br
        obi[pl.ds(rb, _SUBLANES), :] = bi
        return (c0, c1, c2, c3)

    c = lax.fori_loop(0, tiles, body, (carry[0], carry[1], carry[2], carry[3]))
    for q in range(4):
        carry[q] = c[q]


def _s5_out_kernel(x_ref, sc_ref, sh_ref, d_ref, sfr, sfi, sbr, sbi, t_ref, w_ref, o_ref):
    s_refs = (sfr, sfi, sbr, sbi)
    d = d_ref[...]
    for c0, rows in _row_parts(x_ref):
        us = _chunk_tokens(x_ref, sc_ref, sh_ref, c0, rows)
        ys = []
        for h, u in enumerate(_slab_chunks(us)):
            st = jnp.concatenate([s[c0:c0 + rows, h * _CS:(h + 1) * _CS] for s in s_refs], axis=-1)
            ys.append(_dot(u, t_ref[h]) + _dot(st.astype(_bf16), w_ref[h]))
        low = _low_half(rows)
        for t in range(_L):
            cols = slice((t // 2) * _LANES, (t // 2 + 1) * _LANES)
            y_lo, y_hi = ys[0][:, cols], ys[1][:, cols]
            if t % 2 == 0:
                y_t = jnp.where(low, y_lo, pltpu.roll(y_hi, _SLAB_LANES, axis=1))
            else:
                y_t = jnp.where(low, pltpu.roll(y_lo, _SLAB_LANES, axis=1), y_hi)
            o_ref[pl.ds(c0 * _L + t, rows, stride=_L), :] = jax.nn.gelu(d * us[t] + y_t)


def _glu_ln_kernel(y_ref, x_ref, w_ref, b_ref, gate_ref, lng_ref, lnb_ref, o_ref):
    half = y_ref.shape[0] // 2
    for r in (slice(0, half), slice(half, 2 * half)):
        ag = _dot(y_ref[r, :].astype(_bf16), w_ref[...]) + b_ref[...]
        mix = ag[:, :_D] * jax.nn.sigmoid(ag[:, _D:])
        z = _ALPHA * x_ref[r, :] + gate_ref[...] * mix
        o_ref[r, :] = _layer_norm(z, lng_ref[...], lnb_ref[...])


def _s5_layer(x, sc, sh, gate, tables, d_skip, w_glu, b_glu, ln_g, ln_b):
    bsz, slen, _ = x.shape
    cs = slen // _L
    nt = bsz * slen
    toep, w_state, w_out, scan_tab = tables
    rows = min(cs, _S5_ROWS)
    bpc = cs // rows
    nblk = bsz * bpc
    x2 = x.reshape(nt, _D)
    sc3, sh3 = sc.reshape(bsz, 1, _D), sh.reshape(bsz, 1, _D)

    x_spec = pl.BlockSpec((rows * _L, _LANES), lambda q, i: (i, q))
    mod_spec = pl.BlockSpec((None, 1, _LANES), lambda q, i: (i // bpc, 0, q))
    st_spec = pl.BlockSpec((None, rows, _TS), lambda q, i: (i // bpc, i % bpc, q))
    st_shape = jax.ShapeDtypeStruct((bsz, cs, _STATE_W), _f32)
    slab_spec = pl.BlockSpec((2, _CW, _CW), lambda q, i: (q, 0, 0))

    loc = pl.pallas_call(
        _s5_state_kernel,
        grid=(_N_TILES, nblk),
        in_specs=[x_spec, mod_spec, mod_spec, slab_spec],
        out_specs=[st_spec] * 4,
        out_shape=[st_shape] * 4,
        compiler_params=_params(("arbitrary", "arbitrary")),
        name="s5_chunk_states",
    )(x2, sc3, sh3, w_state)

    cblk = min(cs, _SCAN_ROWS)
    nsb = cs // cblk
    fwd_spec = pl.BlockSpec((None, cblk, _TS), lambda b, q, j: (b, j, q))
    bwd_spec = pl.BlockSpec((None, cblk, _TS), lambda b, q, j: (b, nsb - 1 - j, q))
    sin = pl.pallas_call(
        _s5_scan_kernel,
        grid=(bsz, _N_TILES, nsb),
        in_specs=[fwd_spec, fwd_spec, bwd_spec, bwd_spec,
                  pl.BlockSpec((2, 5, 2, _SUBLANES, _TS), lambda b, q, j: (0, 0, 0, 0, q))],
        out_specs=[fwd_spec, fwd_spec, bwd_spec, bwd_spec],
        out_shape=[st_shape] * 4,
        scratch_shapes=[pltpu.VMEM((4, _SUBLANES, _TS), _f32)],
        compiler_params=_params(("arbitrary", "arbitrary", "arbitrary")),
        name="s5_chunk_scan",
    )(*loc, scan_tab)

    y = pl.pallas_call(
        _s5_out_kernel,
        grid=(_N_TILES, nblk),
        in_specs=[x_spec, mod_spec, mod_spec, pl.BlockSpec((1, _LANES), lambda q, i: (0, q)),
                  st_spec, st_spec, st_spec, st_spec, slab_spec, slab_spec],
        out_specs=x_spec,
        out_shape=jax.ShapeDtypeStruct((nt, _D), _f32),
        compiler_params=_params(("arbitrary", "arbitrary")),
        name="s5_chunk_out",
    )(x2, sc3, sh3, d_skip.reshape(1, _D), *sin, toep, w_out)

    bps = slen // _TM
    row_spec = pl.BlockSpec((_TM, _D), lambda i: (i, 0))
    seq_spec = pl.BlockSpec((None, 1, _D), lambda i: (i // bps, 0, 0))
    out = pl.pallas_call(
        _glu_ln_kernel,
        grid=(nt // _TM,),
        in_specs=[row_spec, row_spec, _const_spec((_D, 2 * _D)), _const_spec((1, 2 * _D)),
                  seq_spec, _const_spec((1, _D)), _const_spec((1, _D))],
        out_specs=row_spec,
        out_shape=jax.ShapeDtypeStruct((nt, _D), _f32),
        compiler_params=_params(("arbitrary",)),
        name="s5_glu_ln",
    )(y, x2, w_glu, b_glu.reshape(1, 2 * _D), gate.reshape(bsz, 1, _D),
      ln_g.reshape(1, _D), ln_b.reshape(1, _D))
    return out.reshape(bsz, slen, _D)


def _ffn_kernel(xp_ref, x_ref, xn_ref, sc_ref, sh_ref, gate_ref, win_ref, bin_ref, cw_ref, cb_ref,
                wout_ref, bout_ref, lng_ref, lnb_ref, o_ref, *, blocks_per_seq):
    i = pl.program_id(0)
    first = (i % blocks_per_seq) == 0
    last = (i % blocks_per_seq) == blocks_per_seq - 1
    halo = _BF16_ROWS
    tm = x_ref.shape[0]
    rows = tm + 2 * halo

    sc1 = 1.0 + sc_ref[...]
    sh = sh_ref[...]
    x = x_ref[...]
    h = (x * sc1 + sh).astype(_bf16)
    h_prev = (xp_ref[...] * sc1 + sh).astype(_bf16)
    h_next = (xn_ref[...] * sc1 + sh).astype(_bf16)
    h_ext = jnp.concatenate([h_prev, h, h_next], axis=0)

    rid = lax.broadcasted_iota(jnp.int32, (rows, 1), 0)
    pad = (first & (rid == halo - 1)) | (last & (rid == halo + tm))

    splits = list(zip(_FFN_SPLIT[:-1], _FFN_SPLIT[1:]))
    acts = []
    for lo, hi in splits:
        u = _dot(h_ext, win_ref[:, lo:hi]) + bin_ref[:, lo:hi]
        u = jnp.where(pad, 0.0, u)
        g = _dot(h, win_ref[:, _D_FF + lo:_D_FF + hi]) + bin_ref[:, _D_FF + lo:_D_FF + hi]
        up = pltpu.roll(u, 1, axis=0)[halo:halo + tm]
        dn = pltpu.roll(u, rows - 1, axis=0)[halo:halo + tm]
        conv = (up * cw_ref[0:1, lo:hi] + u[halo:halo + tm] * cw_ref[1:2, lo:hi]
                + dn * cw_ref[2:3, lo:hi] + cb_ref[:, lo:hi])
        acts.append((jax.nn.gelu(conv) * g).astype(_bf16))
    half = tm // 2
    for r in (slice(0, half), slice(half, tm)):
        y = bout_ref[...] + sum(_dot(a[r, :], wout_ref[lo:hi, :]) for a, (lo, hi) in zip(acts, splits))
        z = _ALPHA * x[r, :] + gate_ref[...] * y
        o_ref[r, :] = _layer_norm(z, lng_ref[...], lnb_ref[...])


def _ffn_layer(x, sc, sh, gate, w_in, b_in, conv_w, conv_b, w_out, b_out, ln_g, ln_b):
    bsz, slen, _ = x.shape
    nt = bsz * slen
    bps = slen // _FFN_TM
    hb = _FFN_TM // _BF16_ROWS
    n_halo = nt // _BF16_ROWS
    row_spec = pl.BlockSpec((_FFN_TM, _D), lambda i: (i, 0))
    prev_spec = pl.BlockSpec((_BF16_ROWS, _D), lambda i: (jnp.maximum(i * hb - 1, 0), 0))
    next_spec = pl.BlockSpec((_BF16_ROWS, _D), lambda i: (jnp.minimum((i + 1) * hb, n_halo - 1), 0))
    seq_spec = pl.BlockSpec((None, 1, _D), lambda i: (i // bps, 0, 0))
    x2 = x.reshape(nt, _D)
    out = pl.pallas_call(
        functools.partial(_ffn_kernel, blocks_per_seq=bps),
        grid=(nt // _FFN_TM,),
        in_specs=[prev_spec, row_spec, next_spec, seq_spec, seq_spec, seq_spec,
                  _const_spec((_D, 2 * _D_FF)), _const_spec((1, 2 * _D_FF)),
                  _const_spec((3, _D_FF)), _const_spec((1, _D_FF)),
                  _const_spec((_D_FF, _D)), _const_spec((1, _D)),
                  _const_spec((1, _D)), _const_spec((1, _D))],
        out_specs=row_spec,
        out_shape=jax.ShapeDtypeStruct((nt, _D), _f32),
        compiler_params=_params(("arbitrary",)),
        name="conv_ffn_ln",
    )(x2, x2, x2, sc.reshape(bsz, 1, _D), sh.reshape(bsz, 1, _D), gate.reshape(bsz, 1, _D),
      w_in, b_in.reshape(1, 2 * _D_FF), conv_w, conv_b.reshape(1, _D_FF),
      w_out, b_out.reshape(1, _D), ln_g.reshape(1, _D), ln_b.reshape(1, _D))
    return out.reshape(bsz, slen, _D)


def _qkv_kernel(x_ref, sc_ref, sh_ref, w_ref, b_ref, q_ref, k_ref, v_ref):
    h = (x_ref[...] * (1.0 + sc_ref[...]) + sh_ref[...]).astype(_bf16)
    qkv = _dot(h, w_ref[...]) + b_ref[...]
    q_ref[...] = (qkv[:, :_D] * (_HEAD_DIM ** -0.5)).astype(_bf16)
    k_ref[...] = qkv[:, _D:2 * _D].astype(_bf16)
    v_ref[...] = qkv[:, 2 * _D:].astype(_bf16)


def _bias_table(rpb):
    col = np.arange(_GRID_W)
    start = np.clip(col - _WIN_W // 2, 0, _GRID_W - _WIN_W)
    kc = col[None, :]
    inside = (kc >= start[:, None]) & (kc < start[:, None] + _WIN_W)
    off = kc - col[:, None] + (_WIN_W - 1)
    pick = ((off[None] == np.arange(2 * _WIN_W - 1)[:, None, None]) & inside[None]).astype(np.float32)
    t2 = (jnp.einsum('hrc,cqk->hrqk', rpb, pick, precision=lax.Precision.HIGHEST)
          + np.where(inside, 0.0, _NEG).astype(np.float32))
    return jnp.concatenate([t2[:, :-1], t2[:, 1:]], axis=-1)


def _kv_window_start(jb, grid_rows):
    return jnp.clip(_Q_ROWS * jb - _WIN_H // 2, 0, grid_rows - 2 * _Q_ROWS)


def _attn_kernel(q_ref, k_ref, v_ref, t_ref, x_ref, gate_ref,
                 wo_ref, bo_ref, lng_ref, lnb_ref, o_ref, att, *, grid_rows):
    jb = pl.program_id(1)
    w0 = _kv_window_start(jb, grid_rows)

    lane = lax.broadcasted_iota(jnp.int32, (_GRID_W, 2 * _HEAD_DIM), 1)
    low = lane < _HEAD_DIM
    n_pairs = _N_HEADS // 2

    def row_body(i, carry):
        r = _Q_ROWS * jb + i
        rs = jnp.clip(r - _WIN_H // 2, 0, grid_rows - _WIN_H)
        off = pl.multiple_of((rs - w0) * _GRID_W, _GRID_W)
        ro0 = rs - r + (_WIN_H - 1)
        qrow = pl.multiple_of(i * _GRID_W, _GRID_W)
        cols = [slice(hp * 2 * _HEAD_DIM, (hp + 1) * 2 * _HEAD_DIM) for hp in range(n_pairs)]

        def scores(hp):
            q2 = q_ref[pl.ds(qrow, _GRID_W), cols[hp]]
            k2 = k_ref[pl.ds(off, _BAND), cols[hp]]
            out = []
            for e in range(2):
                qe = jnp.where(low if e == 0 else ~low, q2, jnp.zeros_like(q2))
                out.append(lax.dot_general(qe, k2, (((1,), (1,)), ((), ())), preferred_element_type=_f32))
            return out

        def softmax(hp, ss):
            out = []
            for e, s in enumerate(ss):
                bias = jnp.concatenate([t_ref[2 * hp + e, ro0 + 2 * m] for m in range(_WIN_H // 2)], axis=-1)
                s = s + bias
                p = jnp.exp(s - jnp.max(s, axis=-1, keepdims=True))
                out.append((p.astype(_bf16), jnp.sum(p, axis=-1, keepdims=True)))
            return out

        def values(hp, ps):
            v2 = v_ref[pl.ds(off, _BAND), cols[hp]]
            o0 = _dot(ps[0][0], v2) / ps[0][1]
            o1 = _dot(ps[1][0], v2) / ps[1][1]
            att[pl.ds(qrow, _GRID_W), cols[hp]] = jnp.where(low, o0, o1)

        s_q, p_q = {}, {}
        for n in range(n_pairs + 2 * _ATT_SKEW):
            if n < n_pairs:
                s_q[n] = scores(n)
            if 0 <= n - _ATT_SKEW < n_pairs:
                p_q[n - _ATT_SKEW] = softmax(n - _ATT_SKEW, s_q.pop(n - _ATT_SKEW))
            if 0 <= n - 2 * _ATT_SKEW < n_pairs:
                values(n - 2 * _ATT_SKEW, p_q.pop(n - 2 * _ATT_SKEW))
        return carry

    lax.fori_loop(0, _Q_ROWS, row_body, 0)

    y = _dot(att[...].astype(_bf16), wo_ref[...]) + bo_ref[...]
    z = _ALPHA * x_ref[...] + gate_ref[...] * y
    o_ref[...] = _layer_norm(z, lng_ref[...], lnb_ref[...])


def _attn_layer(x, sc, sh, gate, w_qkv, b_qkv, rpb, w_o, b_o, ln_g, ln_b):
    bsz, slen, _ = x.shape
    nt = bsz * slen
    bps = slen // _TM
    x2 = x.reshape(nt, _D)
    row_spec = pl.BlockSpec((_TM, _D), lambda i: (i, 0))
    seq_spec = pl.BlockSpec((None, 1, _D), lambda i: (i // bps, 0, 0))
    qkv_shape = jax.ShapeDtypeStruct((nt, _D), _bf16)
    q, k, v = pl.pallas_call(
        _qkv_kernel,
        grid=(nt // _TM,),
        in_specs=[row_spec, seq_spec, seq_spec, _const_spec((_D, 3 * _D)), _const_spec((1, 3 * _D))],
        out_specs=[row_spec] * 3,
        out_shape=[qkv_shape] * 3,
        compiler_params=_params(("arbitrary",)),
        name="attn_qkv",
    )(x2, sc.reshape(bsz, 1, _D), sh.reshape(bsz, 1, _D), w_qkv, b_qkv.reshape(1, 3 * _D))

    grid_rows = slen // _GRID_W
    nqb = grid_rows // _Q_ROWS
    blk = _Q_ROWS * _GRID_W
    cur = pl.BlockSpec((blk, _D), lambda b, j: (b * nqb + j, 0))
    window = pl.BlockSpec((pl.Element(2 * blk), pl.Element(_D)),
                          lambda b, j: (pl.multiple_of((b * grid_rows + _kv_window_start(j, grid_rows)) * _GRID_W,
                                                       _GRID_W), 0))
    bseq = pl.BlockSpec((None, 1, _D), lambda b, j: (b, 0, 0))
    table = _bias_table(rpb)
    out = pl.pallas_call(
        functools.partial(_attn_kernel, grid_rows=grid_rows),
        grid=(bsz, nqb),
        in_specs=[cur, window, window, _const_spec(table.shape), cur, bseq,
                  _const_spec((_D, _D)), _const_spec((1, _D)), _const_spec((1, _D)), _const_spec((1, _D))],
        out_specs=cur,
        out_shape=jax.ShapeDtypeStruct((nt, _D), _f32),
        scratch_shapes=[pltpu.VMEM((blk, _D), _f32)],
        compiler_params=_params(("arbitrary", "arbitrary")),
        name="nbr_attn_ln",
    )(q, k, v, table, x2, gate.reshape(bsz, 1, _D), w_o, b_o.reshape(1, _D),
      ln_g.reshape(1, _D), ln_b.reshape(1, _D))
    return out.reshape(bsz, slen, _D)


def _trunk(x, mod, s5_tables, s5_d, w_glu, b_glu, w_qkv, b_qkv, rpb, w_o, b_o,
           ffn_w_in, ffn_b_in, ffn_conv_w, ffn_conv_b, ffn_w_out, ffn_b_out, ln_g, ln_b):
    for i in range(_DEPTH):
        sh_m, sc_m, g_m, sh_f, sc_f, g_f = [mod[i, :, n * _D:(n + 1) * _D] for n in range(6)]
        if i % 2 == 0:
            x = _s5_layer(x, sc_m, sh_m, g_m, s5_tables, s5_d, w_glu, b_glu, ln_g[i, 0], ln_b[i, 0])
        else:
            x = _attn_layer(x, sc_m, sh_m, g_m, w_qkv, b_qkv, rpb, w_o, b_o, ln_g[i, 0], ln_b[i, 0])
        x = _ffn_layer(x, sc_f, sh_f, g_f, ffn_w_in[i], ffn_b_in[i], ffn_conv_w[i], ffn_conv_b[i],
                       ffn_w_out[i], ffn_b_out[i], ln_g[i, 1], ln_b[i, 1])
    return x


def kernel(x_prompt, x_sample, c_prompt, c_sample, w_ada, b_ada, ln_g, ln_b, s5_lam_re, s5_lam_im, s5_log_dt, s5_b_re, s5_b_im, s5_c_re, s5_c_im, s5_d, s5_w_glu, s5_b_glu, na_w_qkv, na_b_qkv, na_rpb, na_w_o, na_b_o, ffn_w_in, ffn_b_in, ffn_conv_w, ffn_conv_b, ffn_w_out, ffn_b_out):
    assert _DEPTH == 2 and x_prompt.shape[-1] == _D
    nb_p, nb_s = c_prompt.shape[0], c_sample.shape[0]
    nb_pad = -(-(nb_p + nb_s) // _SUBLANES) * _SUBLANES
    c_all = jnp.zeros((nb_pad, _D), _f32).at[:nb_p].set(c_prompt).at[nb_p:nb_p + nb_s].set(c_sample)
    mod = _modulation(c_all, w_ada, b_ada)

    tables = _s5_slab_tables(s5_lam_re[0], s5_lam_im[0], s5_log_dt[0], s5_b_re[0], s5_b_im[0],
                        s5_c_re[0], s5_c_im[0])
    shared = (tables, s5_d[0], s5_w_glu[0].astype(_bf16), s5_b_glu[0],
              na_w_qkv[0].astype(_bf16), na_b_qkv[0], na_rpb[0], na_w_o[0].astype(_bf16), na_b_o[0],
              ffn_w_in.astype(_bf16), ffn_b_in, ffn_conv_w, ffn_conv_b, ffn_w_out.astype(_bf16), ffn_b_out,
              ln_g, ln_b)
    y_prompt = _trunk(x_prompt, mod[:, :nb_p], *shared)
    y_sample = _trunk(x_sample, mod[:, nb_p:nb_p + nb_s], *shared)
    return (y_prompt, y_sample)
```

```python
import functools

import numpy as np
import jax
import jax.numpy as jnp
from jax import lax
from jax.experimental import pallas as pl
from jax.experimental.pallas import tpu as pltpu

_D = 1024
_DEPTH = 2
_GRID_W = 64
_GROUP_DIM = 16
_N_GROUPS = _D // _GROUP_DIM
_STATE_DIM = 64
_N_HEADS = 16
_HEAD_DIM = _D // _N_HEADS
_WIN_H = 8
_WIN_W = 16
_D_FF = 2816
_ALPHA = (2 * _DEPTH) ** 0.25
_LN_EPS = 1e-5

_LANES = 128
_SUBLANES = 8
_BF16_ROWS = 16
_VMEM_LIMIT = 56 * 1024 * 1024

_L = 16
_GW = _L * _GROUP_DIM
_SG = 4
_SLAB_LANES = _SG * _GROUP_DIM
_N_SLABS = _N_GROUPS // _SG
_N_TILES = _D // _LANES
_CW = _L * _SLAB_LANES
_CS = _SG * _STATE_DIM
_TS = 2 * _CS
_STATE_W = _N_GROUPS * _STATE_DIM
_S5_ROWS = 256
_S5_PARTS = 1
_SCAN_ROWS = 256

_TM = 512
_FFN_TM = 1024
_FFN_SPLIT = (0, 1024, 2048, _D_FF)
_Q_ROWS = 8
_BAND = _WIN_H * _GRID_W
_ATT_SKEW = 8
_NEG = -1e30

_f32 = jnp.float32
_bf16 = jnp.bfloat16


def _params(sem):
    return pltpu.CompilerParams(dimension_semantics=sem, vmem_limit_bytes=_VMEM_LIMIT)


def _const_spec(shape):
    nd = len(shape)
    return pl.BlockSpec(shape, lambda *_: (0,) * nd, pipeline_mode=pl.Buffered(1))


def _layer_norm(z, g, b):
    mu = jnp.mean(z, axis=-1, keepdims=True)
    d = z - mu
    var = jnp.mean(d * d, axis=-1, keepdims=True)
    return d * lax.rsqrt(var + _LN_EPS) * g + b


def _dot(a, b):
    return jnp.dot(a, b, preferred_element_type=_f32)


def _mod_kernel(c_ref, w_ref, b_ref, o_ref):
    cond = jax.nn.silu(c_ref[...]).astype(_bf16)
    o_ref[...] = _dot(cond, w_ref[...].astype(_bf16)) + b_ref[...]


def _modulation(c_all, w_ada, b_ada):
    nb = c_all.shape[0]
    tn = 1536
    return pl.pallas_call(
        _mod_kernel,
        grid=(_DEPTH, 6 * _D // tn),
        in_specs=[
            pl.BlockSpec((nb, _D), lambda l, n: (0, 0)),
            pl.BlockSpec((None, _D, tn), lambda l, n: (l, 0, n)),
            pl.BlockSpec((None, 1, tn), lambda l, n: (l, 0, n)),
        ],
        out_specs=pl.BlockSpec((None, nb, tn), lambda l, n: (l, 0, n)),
        out_shape=jax.ShapeDtypeStruct((_DEPTH, nb, 6 * _D), _f32),
        compiler_params=_params(("arbitrary", "arbitrary")),
        name="adaln_mod",
    )(c_all, w_ada, b_ada.reshape(_DEPTH, 1, 6 * _D))


def _s5_tables(lam_re, lam_im, log_dt, b_re, b_im, c_re, c_im):
    hi = lax.Precision.HIGHEST
    dt = jnp.exp(log_dt)[..., None]
    z_re, z_im = lam_re * dt, lam_im * dt
    mag = jnp.exp(z_re)
    ab_re, ab_im = mag * jnp.cos(z_im), mag * jnp.sin(z_im)
    den = lam_re * lam_re + lam_im * lam_im
    nr, ni = ab_re - 1.0, ab_im
    f_re = (nr * lam_re + ni * lam_im) / den
    f_im = (ni * lam_re - nr * lam_im) / den
    bb_re = f_re[..., None] * b_re - f_im[..., None] * b_im
    bb_im = f_re[..., None] * b_im + f_im[..., None] * b_re
    bb_re = bb_re.transpose(0, 3, 1, 2).reshape(2, _GROUP_DIM, _STATE_W)
    bb_im = bb_im.transpose(0, 3, 1, 2).reshape(2, _GROUP_DIM, _STATE_W)

    k = jnp.arange(_L + 1, dtype=_f32)[:, None, None]
    zf_re, zf_im = z_re.reshape(2, _STATE_W), z_im.reshape(2, _STATE_W)
    pw_mag = jnp.exp(k * zf_re)
    pw_re, pw_im = pw_mag * jnp.cos(k * zf_im), pw_mag * jnp.sin(k * zf_im)

    abb_re = pw_re[:, :, None, :] * bb_re - pw_im[:, :, None, :] * bb_im
    abb_im = pw_re[:, :, None, :] * bb_im + pw_im[:, :, None, :] * bb_re

    by_group = (_L, 2, _GROUP_DIM, _N_GROUPS, _STATE_DIM)
    lag = (jnp.einsum('kdhgp,dgop->kdgho', abb_re[:_L].reshape(by_group), c_re, precision=hi)
           - jnp.einsum('kdhgp,dgop->kdgho', abb_im[:_L].reshape(by_group), c_im, precision=hi))
    s_idx = np.arange(_L)[:, None]
    t_idx = np.arange(_L)[None, :]
    k_idx = np.arange(_L)[:, None, None]
    place = np.stack([t_idx - s_idx == k_idx, s_idx - t_idx == k_idx], axis=1).astype(np.float32)
    toep = jnp.einsum('kdst,kdgho->gshto', place, lag, precision=hi).reshape(_N_GROUPS, _GW, _GW)

    rev = np.arange(_L - 1, -1, -1)
    st = jnp.stack([abb_re[rev, 0], abb_im[rev, 0], abb_re[:_L, 1], abb_im[:_L, 1]], axis=0)
    w_state = st.reshape(4, _GW, _STATE_W)

    ct_re = c_re.transpose(0, 2, 1, 3).reshape(2, _GROUP_DIM, _STATE_W)
    ct_im = c_im.transpose(0, 2, 1, 3).reshape(2, _GROUP_DIM, _STATE_W)

    def out_mat(d, powers):
        pr_, pi_ = pw_re[powers, d][:, None, :], pw_im[powers, d][:, None, :]
        wr = ct_re[d] * pr_ - ct_im[d] * pi_
        wi = ct_re[d] * pi_ + ct_im[d] * pr_
        return wr, -wi

    wfr, wfi = out_mat(0, np.arange(1, _L + 1))
    wbr, wbi = out_mat(1, np.arange(_L, 0, -1))
    w_out = jnp.stack([wfr, wfi, wbr, wbi], axis=0).reshape(4, _GW, _STATE_W)

    flow = np.stack([np.arange(_SUBLANES), np.arange(_SUBLANES - 1, -1, -1)])[:, :, None, None]

    def chunk_pow(m):
        e = jnp.asarray(m, _f32) * float(_L)
        pm = jnp.exp(e * z_re[:, None])
        return jnp.stack([pm * jnp.cos(e * z_im[:, None]), pm * jnp.sin(e * z_im[:, None])], axis=1)

    def masked(d):
        return chunk_pow(np.full_like(flow, d)) * jnp.asarray(flow >= d, _f32)[:, None]

    scan_tab = jnp.stack([chunk_pow(np.ones_like(flow)), masked(1), masked(2), masked(4), chunk_pow(flow)], axis=1)
    scan_tab = scan_tab.reshape(2, 5, 2, _SUBLANES, _STATE_W)
    return toep.astype(_bf16), w_state, w_out, scan_tab


def _expand_kernel(a_ref, o_ref, *, row_run, col_run):
    n = a_ref.shape[1]
    for g in range(_SG):
        xg = _dot(a_ref[g], _group_spread(n, col_run, g)).astype(_bf16)
        for b in range(n // row_run):
            r0 = (b * _SG + g) * row_run
            o_ref[r0:r0 + row_run, :] = xg[b * row_run:(b + 1) * row_run, :]


def _expand(per_group, row_run, col_run):
    n = per_group.shape[1]
    return pl.pallas_call(
        functools.partial(_expand_kernel, row_run=row_run, col_run=col_run),
        grid=(_N_SLABS,),
        in_specs=[pl.BlockSpec((_SG, n, n), lambda q: (q, 0, 0))],
        out_specs=pl.BlockSpec((None, _SG * n, _SG * n), lambda q: (q, 0, 0)),
        out_shape=jax.ShapeDtypeStruct((_N_SLABS, _SG * n, _SG * n), _bf16),
        compiler_params=_params(("arbitrary",)),
        name="s5_expand",
    )(per_group)


def _group_spread(n, run, g):
    j = lax.broadcasted_iota(jnp.int32, (n, _SG * n), 0)
    c = lax.broadcasted_iota(jnp.int32, (n, _SG * n), 1)
    shift = run.bit_length() - 1
    spread = lax.shift_left(lax.shift_right_logical(j, shift), shift + _SG.bit_length() - 1) + (j & (run - 1))
    return jnp.where(c == spread + g * run, 1.0, 0.0).astype(_bf16)


def _expand_state_kernel(a_ref, o_ref):
    lane_group = lax.shift_right_logical(
        lax.broadcasted_iota(jnp.int32, (_GROUP_DIM, _CS), 1), _STATE_DIM.bit_length() - 1)
    for c in range(4):
        for g in range(_SG):
            for s in range(_L):
                blk = a_ref[c, s * _GROUP_DIM:(s + 1) * _GROUP_DIM, :]
                r0 = (s * _SG + g) * _GROUP_DIM
                o_ref[r0:r0 + _GROUP_DIM, c * _CS:(c + 1) * _CS] = jnp.where(lane_group == g, blk, 0.0).astype(_bf16)


def _expand_out_kernel(a_ref, o_ref):
    for c in range(4):
        xt = a_ref[c].T.astype(_bf16)
        for g in range(_SG):
            rows = slice(g * _STATE_DIM, (g + 1) * _STATE_DIM)
            placed = _dot(xt[rows, :], _group_spread(_GW, _GROUP_DIM, g))
            o_ref[c * _CS + g * _STATE_DIM:c * _CS + (g + 1) * _STATE_DIM, :] = placed.astype(_bf16)


def _expand_states(kern, table):
    return pl.pallas_call(
        kern,
        grid=(_N_SLABS,),
        in_specs=[pl.BlockSpec((4, _GW, _CS), lambda q: (0, 0, q))],
        out_specs=pl.BlockSpec((None, _CW, _CW), lambda q: (q, 0, 0)),
        out_shape=jax.ShapeDtypeStruct((_N_SLABS, _CW, _CW), _bf16),
        compiler_params=_params(("arbitrary",)),
        name="s5_expand_states",
    )(table)


def _s5_slab_tables(*s5_params):
    toep, w_state, w_out, scan_tab = _s5_tables(*s5_params)
    return (_expand(toep, _GROUP_DIM, _GROUP_DIM), _expand_states(_expand_state_kernel, w_state),
            _expand_states(_expand_out_kernel, w_out), scan_tab)


def _chunk_tokens(x_ref, sc_ref, sh_ref, c0, rows):
    sc1 = 1.0 + sc_ref[...]
    sh = sh_ref[...]
    return [x_ref[pl.ds(c0 * _L + s, rows, stride=_L), :] * sc1 + sh for s in range(_L)]


def _low_half(rows):
    return lax.broadcasted_iota(jnp.int32, (rows, _LANES), 1) < _SLAB_LANES


def _slab_chunks(us):
    low = _low_half(us[0].shape[0])
    lo, hi = [], []
    for m in range(_L // 2):
        a, b = us[2 * m], us[2 * m + 1]
        lo.append(jnp.where(low, a, pltpu.roll(b, _SLAB_LANES, axis=1)))
        hi.append(jnp.where(low, pltpu.roll(a, _SLAB_LANES, axis=1), b))
    return (jnp.concatenate(lo, axis=-1).astype(_bf16), jnp.concatenate(hi, axis=-1).astype(_bf16))


def _row_parts(x_ref):
    rows = x_ref.shape[0] // _L
    part = rows // _S5_PARTS
    return [(p * part, part) for p in range(_S5_PARTS)]


def _s5_state_kernel(x_ref, sc_ref, sh_ref, w_ref, *o_refs):
    for c0, rows in _row_parts(x_ref):
        us = _chunk_tokens(x_ref, sc_ref, sh_ref, c0, rows)
        for h, u in enumerate(_slab_chunks(us)):
            r = _dot(u, w_ref[h])
            for c, o_ref in enumerate(o_refs):
                o_ref[c0:c0 + rows, h * _CS:(h + 1) * _CS] = r[:, c * _CS:(c + 1) * _CS]


def _s5_scan_kernel(lfr, lfi, lbr, lbi, tab_ref, ofr, ofi, obr, obi, carry):
    @pl.when(pl.program_id(2) == 0)
    def _():
        carry[...] = jnp.zeros_like(carry)

    tiles = lfr.shape[0] // _SUBLANES
    sub = lax.broadcasted_iota(jnp.int32, (_SUBLANES, lfr.shape[1]), 0)

    def cmul(ar, ai, br, bi):
        return ar * br - ai * bi, ar * bi + ai * br

    def tile_scan(xr, xi, f0r, f0i, d):
        def shift(v, k):
            return pltpu.roll(v, k if d == 0 else _SUBLANES - k, axis=0)

        ir, ii = xr, xi
        for n, k in ((1, 1), (2, 2), (3, 4)):
            pr, pi = cmul(tab_ref[d, n, 0], tab_ref[d, n, 1], shift(ir, k), shift(ii, k))
            ir, ii = ir + pr, ii + pi
        first = sub == (0 if d == 0 else _SUBLANES - 1)
        cr, ci = cmul(tab_ref[d, 4, 0], tab_ref[d, 4, 1], f0r, f0i)
        fin_r = cr + jnp.where(first, 0.0, shift(ir, 1))
        fin_i = ci + jnp.where(first, 0.0, shift(ii, 1))
        nr, ni = cmul(tab_ref[d, 0, 0], tab_ref[d, 0, 1], fin_r, fin_i)
        nr, ni = nr + xr, ni + xi
        end = _SUBLANES - 1 if d == 0 else 0
        return (fin_r, fin_i, jnp.broadcast_to(nr[end:end + 1], nr.shape),
                jnp.broadcast_to(ni[end:end + 1], ni.shape))

    def body(i, c):
        rf = pl.multiple_of(i * _SUBLANES, _SUBLANES)
        rb = pl.multiple_of((tiles - 1 - i) * _SUBLANES, _SUBLANES)
        fr, fi, c0, c1 = tile_scan(lfr[pl.ds(rf, _SUBLANES), :], lfi[pl.ds(rf, _SUBLANES), :], c[0], c[1], 0)
        ofr[pl.ds(rf, _SUBLANES), :] = fr
        ofi[pl.ds(rf, _SUBLANES), :] = fi
        br, bi, c2, c3 = tile_scan(lbr[pl.ds(rb, _SUBLANES), :], lbi[pl.ds(rb, _SUBLANES), :], c[2], c[3], 1)
        obr[pl.ds(rb, _SUBLANES), :] = br
        obi[pl.ds(rb, _SUBLANES), :] = bi
        return (c0, c1, c2, c3)

    c = lax.fori_loop(0, tiles, body, (carry[0], carry[1], carry[2], carry[3]))
    for q in range(4):
        carry[q] = c[q]


def _s5_out_kernel(x_ref, sc_ref, sh_ref, d_ref, sfr, sfi, sbr, sbi, t_ref, w_ref, o_ref):
    s_refs = (sfr, sfi, sbr, sbi)
    d = d_ref[...]
    for c0, rows in _row_parts(x_ref):
        us = _chunk_tokens(x_ref, sc_ref, sh_ref, c0, rows)
        ys = []
        for h, u in enumerate(_slab_chunks(us)):
            st = jnp.concatenate([s[c0:c0 + rows, h * _CS:(h + 1) * _CS] for s in s_refs], axis=-1)
            ys.append(_dot(u, t_ref[h]) + _dot(st.astype(_bf16), w_ref[h]))
        low = _low_half(rows)
        for t in range(_L):
            cols = slice((t // 2) * _LANES, (t // 2 + 1) * _LANES)
            y_lo, y_hi = ys[0][:, cols], ys[1][:, cols]
            if t % 2 == 0:
                y_t = jnp.where(low, y_lo, pltpu.roll(y_hi, _SLAB_LANES, axis=1))
            else:
                y_t = jnp.where(low, pltpu.roll(y_lo, _SLAB_LANES, axis=1), y_hi)
            o_ref[pl.ds(c0 * _L + t, rows, stride=_L), :] = jax.nn.gelu(d * us[t] + y_t)


def _glu_ln_kernel(y_ref, x_ref, w_ref, b_ref, gate_ref, lng_ref, lnb_ref, o_ref):
    half = y_ref.shape[0] // 2
    for r in (slice(0, half), slice(half, 2 * half)):
        ag = _dot(y_ref[r, :].astype(_bf16), w_ref[...]) + b_ref[...]
        mix = ag[:, :_D] * jax.nn.sigmoid(ag[:, _D:])
        z = _ALPHA * x_ref[r, :] + gate_ref[...] * mix
        o_ref[r, :] = _layer_norm(z, lng_ref[...], lnb_ref[...])


def _s5_layer(x, sc, sh, gate, tables, d_skip, w_glu, b_glu, ln_g, ln_b):
    bsz, slen, _ = x.shape
    cs = slen // _L
    nt = bsz * slen
    toep, w_state, w_out, scan_tab = tables
    rows = min(cs, _S5_ROWS)
    bpc = cs // rows
    nblk = bsz * bpc
    x2 = x.reshape(nt, _D)
    sc3, sh3 = sc.reshape(bsz, 1, _D), sh.reshape(bsz, 1, _D)

    x_spec = pl.BlockSpec((rows * _L, _LANES), lambda q, i: (i, q))
    mod_spec = pl.BlockSpec((None, 1, _LANES), lambda q, i: (i // bpc, 0, q))
    st_spec = pl.BlockSpec((None, rows, _TS), lambda q, i: (i // bpc, i % bpc, q))
    st_shape = jax.ShapeDtypeStruct((bsz, cs, _STATE_W), _f32)
    slab_spec = pl.BlockSpec((2, _CW, _CW), lambda q, i: (q, 0, 0))

    loc = pl.pallas_call(
        _s5_state_kernel,
        grid=(_N_TILES, nblk),
        in_specs=[x_spec, mod_spec, mod_spec, slab_spec],
        out_specs=[st_spec] * 4,
        out_shape=[st_shape] * 4,
        compiler_params=_params(("arbitrary", "arbitrary")),
        name="s5_chunk_states",
    )(x2, sc3, sh3, w_state)

    cblk = min(cs, _SCAN_ROWS)
    nsb = cs // cblk
    fwd_spec = pl.BlockSpec((None, cblk, _TS), lambda b, q, j: (b, j, q))
    bwd_spec = pl.BlockSpec((None, cblk, _TS), lambda b, q, j: (b, nsb - 1 - j, q))
    sin = pl.pallas_call(
        _s5_scan_kernel,
        grid=(bsz, _N_TILES, nsb),
        in_specs=[fwd_spec, fwd_spec, bwd_spec, bwd_spec,
                  pl.BlockSpec((2, 5, 2, _SUBLANES, _TS), lambda b, q, j: (0, 0, 0, 0, q))],
        out_specs=[fwd_spec, fwd_spec, bwd_spec, bwd_spec],
        out_shape=[st_shape] * 4,
        scratch_shapes=[pltpu.VMEM((4, _SUBLANES, _TS), _f32)],
        compiler_params=_params(("arbitrary", "arbitrary", "arbitrary")),
        name="s5_chunk_scan",
    )(*loc, scan_tab)

    y = pl.pallas_call(
        _s5_out_kernel,
        grid=(_N_TILES, nblk),
        in_specs=[x_spec, mod_spec, mod_spec, pl.BlockSpec((1, _LANES), lambda q, i: (0, q)),
                  st_spec, st_spec, st_spec, st_spec, slab_spec, slab_spec],
        out_specs=x_spec,
        out_shape=jax.ShapeDtypeStruct((nt, _D), _f32),
        compiler_params=_params(("arbitrary", "arbitrary")),
        name="s5_chunk_out",
    )(x2, sc3, sh3, d_skip.reshape(1, _D), *sin, toep, w_out)

    bps = slen // _TM
    row_spec = pl.BlockSpec((_TM, _D), lambda i: (i, 0))
    seq_spec = pl.BlockSpec((None, 1, _D), lambda i: (i // bps, 0, 0))
    out = pl.pallas_call(
        _glu_ln_kernel,
        grid=(nt // _TM,),
        in_specs=[row_spec, row_spec, _const_spec((_D, 2 * _D)), _const_spec((1, 2 * _D)),
                  seq_spec, _const_spec((1, _D)), _const_spec((1, _D))],
        out_specs=row_spec,
        out_shape=jax.ShapeDtypeStruct((nt, _D), _f32),
        compiler_params=_params(("arbitrary",)),
        name="s5_glu_ln",
    )(y, x2, w_glu, b_glu.reshape(1, 2 * _D), gate.reshape(bsz, 1, _D),
      ln_g.reshape(1, _D), ln_b.reshape(1, _D))
    return out.reshape(bsz, slen, _D)


def _ffn_kernel(xp_ref, x_ref, xn_ref, sc_ref, sh_ref, gate_ref, win_ref, bin_ref, cw_ref, cb_ref,
                wout_ref, bout_ref, lng_ref, lnb_ref, o_ref, *, blocks_per_seq):
    i = pl.program_id(0)
    first = (i % blocks_per_seq) == 0
    last = (i % blocks_per_seq) == blocks_per_seq - 1
    halo = _BF16_ROWS
    tm = x_ref.shape[0]
    rows = tm + 2 * halo

    sc1 = 1.0 + sc_ref[...]
    sh = sh_ref[...]
    x = x_ref[...]
    h = (x * sc1 + sh).astype(_bf16)
    h_prev = (xp_ref[...] * sc1 + sh).astype(_bf16)
    h_next = (xn_ref[...] * sc1 + sh).astype(_bf16)
    h_ext = jnp.concatenate([h_prev, h, h_next], axis=0)

    rid = lax.broadcasted_iota(jnp.int32, (rows, 1), 0)
    pad = (first & (rid == halo - 1)) | (last & (rid == halo + tm))

    splits = list(zip(_FFN_SPLIT[:-1], _FFN_SPLIT[1:]))
    acts = []
    for lo, hi in splits:
        u = _dot(h_ext, win_ref[:, lo:hi]) + bin_ref[:, lo:hi]
        u = jnp.where(pad, 0.0, u)
        g = _dot(h, win_ref[:, _D_FF + lo:_D_FF + hi]) + bin_ref[:, _D_FF + lo:_D_FF + hi]
        up = pltpu.roll(u, 1, axis=0)[halo:halo + tm]
        dn = pltpu.roll(u, rows - 1, axis=0)[halo:halo + tm]
        conv = (up * cw_ref[0:1, lo:hi] + u[halo:halo + tm] * cw_ref[1:2, lo:hi]
                + dn * cw_ref[2:3, lo:hi] + cb_ref[:, lo:hi])
        acts.append((jax.nn.gelu(conv) * g).astype(_bf16))
    half = tm // 2
    for r in (slice(0, half), slice(half, tm)):
        y = bout_ref[...] + sum(_dot(a[r, :], wout_ref[lo:hi, :]) for a, (lo, hi) in zip(acts, splits))
        z = _ALPHA * x[r, :] + gate_ref[...] * y
        o_ref[r, :] = _layer_norm(z, lng_ref[...], lnb_ref[...])


def _ffn_layer(x, sc, sh, gate, w_in, b_in, conv_w, conv_b, w_out, b_out, ln_g, ln_b):
    bsz, slen, _ = x.shape
    nt = bsz * slen
    bps = slen // _FFN_TM
    hb = _FFN_TM // _BF16_ROWS
    n_halo = nt // _BF16_ROWS
    row_spec = pl.BlockSpec((_FFN_TM, _D), lambda i: (i, 0))
    prev_spec = pl.BlockSpec((_BF16_ROWS, _D), lambda i: (jnp.maximum(i * hb - 1, 0), 0))
    next_spec = pl.BlockSpec((_BF16_ROWS, _D), lambda i: (jnp.minimum((i + 1) * hb, n_halo - 1), 0))
    seq_spec = pl.BlockSpec((None, 1, _D), lambda i: (i // bps, 0, 0))
    x2 = x.reshape(nt, _D)
    out = pl.pallas_call(
        functools.partial(_ffn_kernel, blocks_per_seq=bps),
        grid=(nt // _FFN_TM,),
        in_specs=[prev_spec, row_spec, next_spec, seq_spec, seq_spec, seq_spec,
                  _const_spec((_D, 2 * _D_FF)), _const_spec((1, 2 * _D_FF)),
                  _const_spec((3, _D_FF)), _const_spec((1, _D_FF)),
                  _const_spec((_D_FF, _D)), _const_spec((1, _D)),
                  _const_spec((1, _D)), _const_spec((1, _D))],
        out_specs=row_spec,
        out_shape=jax.ShapeDtypeStruct((nt, _D), _f32),
        compiler_params=_params(("arbitrary",)),
        name="conv_ffn_ln",
    )(x2, x2, x2, sc.reshape(bsz, 1, _D), sh.reshape(bsz, 1, _D), gate.reshape(bsz, 1, _D),
      w_in, b_in.reshape(1, 2 * _D_FF), conv_w, conv_b.reshape(1, _D_FF),
      w_out, b_out.reshape(1, _D), ln_g.reshape(1, _D), ln_b.reshape(1, _D))
    return out.reshape(bsz, slen, _D)


def _qkv_kernel(x_ref, sc_ref, sh_ref, w_ref, b_ref, q_ref, k_ref, v_ref):
    h = (x_ref[...] * (1.0 + sc_ref[...]) + sh_ref[...]).astype(_bf16)
    qkv = _dot(h, w_ref[...]) + b_ref[...]
    q_ref[...] = (qkv[:, :_D] * (_HEAD_DIM ** -0.5)).astype(_bf16)
    k_ref[...] = qkv[:, _D:2 * _D].astype(_bf16)
    v_ref[...] = qkv[:, 2 * _D:].astype(_bf16)


def _bias_table(rpb):
    col = np.arange(_GRID_W)
    start = np.clip(col - _WIN_W // 2, 0, _GRID_W - _WIN_W)
    kc = col[None, :]
    inside = (kc >= start[:, None]) & (kc < start[:, None] + _WIN_W)
    off = kc - col[:, None] + (_WIN_W - 1)
    pick = ((off[None] == np.arange(2 * _WIN_W - 1)[:, None, None]) & inside[None]).astype(np.float32)
    t2 = (jnp.einsum('hrc,cqk->hrqk', rpb, pick, precision=lax.Precision.HIGHEST)
          + np.where(inside, 0.0, _NEG).astype(np.float32))
    return jnp.concatenate([t2[:, :-1], t2[:, 1:]], axis=-1)


def _kv_window_start(jb, grid_rows):
    return jnp.clip(_Q_ROWS * jb - _WIN_H // 2, 0, grid_rows - 2 * _Q_ROWS)


def _attn_kernel(q_ref, k_ref, v_ref, t_ref, x_ref, gate_ref,
                 wo_ref, bo_ref, lng_ref, lnb_ref, o_ref, att, *, grid_rows):
    jb = pl.program_id(1)
    w0 = _kv_window_start(jb, grid_rows)

    lane = lax.broadcasted_iota(jnp.int32, (_GRID_W, 2 * _HEAD_DIM), 1)
    low = lane < _HEAD_DIM
    n_pairs = _N_HEADS // 2

    def row_body(i, carry):
        r = _Q_ROWS * jb + i
        rs = jnp.clip(r - _WIN_H // 2, 0, grid_rows - _WIN_H)
        off = pl.multiple_of((rs - w0) * _GRID_W, _GRID_W)
        ro0 = rs - r + (_WIN_H - 1)
        qrow = pl.multiple_of(i * _GRID_W, _GRID_W)
        cols = [slice(hp * 2 * _HEAD_DIM, (hp + 1) * 2 * _HEAD_DIM) for hp in range(n_pairs)]

        def scores(hp):
            q2 = q_ref[pl.ds(qrow, _GRID_W), cols[hp]]
            k2 = k_ref[pl.ds(off, _BAND), cols[hp]]
            zero = jnp.zeros_like(q2)
            qq = jnp.concatenate([jnp.where(low, q2, zero), jnp.where(low, zero, q2)], axis=0)
            return lax.dot_general(qq, k2, (((1,), (1,)), ((), ())), preferred_element_type=_f32)

        def softmax(hp, s):
            bias = jnp.concatenate(
                [jnp.concatenate([t_ref[2 * hp + e, ro0 + 2 * m] for m in range(_WIN_H // 2)], axis=-1)
                 for e in range(2)], axis=0)
            s = s + bias
            p = jnp.exp(s - jnp.max(s, axis=-1, keepdims=True))
            return p.astype(_bf16), jnp.sum(p, axis=-1, keepdims=True)

        def values(hp, ps):
            v2 = v_ref[pl.ds(off, _BAND), cols[hp]]
            o = _dot(ps[0], v2) / ps[1]
            att[pl.ds(qrow, _GRID_W), cols[hp]] = jnp.where(low, o[:_GRID_W], o[_GRID_W:])

        s_q, p_q = {}, {}
        for n in range(n_pairs + 2 * _ATT_SKEW):
            if n < n_pairs:
                s_q[n] = scores(n)
            if 0 <= n - _ATT_SKEW < n_pairs:
                p_q[n - _ATT_SKEW] = softmax(n - _ATT_SKEW, s_q.pop(n - _ATT_SKEW))
            if 0 <= n - 2 * _ATT_SKEW < n_pairs:
                values(n - 2 * _ATT_SKEW, p_q.pop(n - 2 * _ATT_SKEW))
        return carry

    lax.fori_loop(0, _Q_ROWS, row_body, 0)

    y = _dot(att[...].astype(_bf16), wo_ref[...]) + bo_ref[...]
    z = _ALPHA * x_ref[...] + gate_ref[...] * y
    o_ref[...] = _layer_norm(z, lng_ref[...], lnb_ref[...])


def _attn_layer(x, sc, sh, gate, w_qkv, b_qkv, rpb, w_o, b_o, ln_g, ln_b):
    bsz, slen, _ = x.shape
    nt = bsz * slen
    bps = slen // _TM
    x2 = x.reshape(nt, _D)
    row_spec = pl.BlockSpec((_TM, _D), lambda i: (i, 0))
    seq_spec = pl.BlockSpec((None, 1, _D), lambda i: (i // bps, 0, 0))
    qkv_shape = jax.ShapeDtypeStruct((nt, _D), _bf16)
    q, k, v = pl.pallas_call(
        _qkv_kernel,
        grid=(nt // _TM,),
        in_specs=[row_spec, seq_spec, seq_spec, _const_spec((_D, 3 * _D)), _const_spec((1, 3 * _D))],
        out_specs=[row_spec] * 3,
        out_shape=[qkv_shape] * 3,
        compiler_params=_params(("arbitrary",)),
        name="attn_qkv",
    )(x2, sc.reshape(bsz, 1, _D), sh.reshape(bsz, 1, _D), w_qkv, b_qkv.reshape(1, 3 * _D))

    grid_rows = slen // _GRID_W
    nqb = grid_rows // _Q_ROWS
    blk = _Q_ROWS * _GRID_W
    cur = pl.BlockSpec((blk, _D), lambda b, j: (b * nqb + j, 0))
    window = pl.BlockSpec((pl.Element(2 * blk), pl.Element(_D)),
                          lambda b, j: (pl.multiple_of((b * grid_rows + _kv_window_start(j, grid_rows)) * _GRID_W,
                                                       _GRID_W), 0))
    bseq = pl.BlockSpec((None, 1, _D), lambda b, j: (b, 0, 0))
    table = _bias_table(rpb)
    out = pl.pallas_call(
        functools.partial(_attn_kernel, grid_rows=grid_rows),
        grid=(bsz, nqb),
        in_specs=[cur, window, window, _const_spec(table.shape), cur, bseq,
                  _const_spec((_D, _D)), _const_spec((1, _D)), _const_spec((1, _D)), _const_spec((1, _D))],
        out_specs=cur,
        out_shape=jax.ShapeDtypeStruct((nt, _D), _f32),
        scratch_shapes=[pltpu.VMEM((blk, _D), _f32)],
        compiler_params=_params(("arbitrary", "arbitrary")),
        name="nbr_attn_ln",
    )(q, k, v, table, x2, gate.reshape(bsz, 1, _D), w_o, b_o.reshape(1, _D),
      ln_g.reshape(1, _D), ln_b.reshape(1, _D))
    return out.reshape(bsz, slen, _D)


def _trunk(x, mod, s5_tables, s5_d, w_glu, b_glu, w_qkv, b_qkv, rpb, w_o, b_o,
           ffn_w_in, ffn_b_in, ffn_conv_w, ffn_conv_b, ffn_w_out, ffn_b_out, ln_g, ln_b):
    for i in range(_DEPTH):
        sh_m, sc_m, g_m, sh_f, sc_f, g_f = [mod[i, :, n * _D:(n + 1) * _D] for n in range(6)]
        if i % 2 == 0:
            x = _s5_layer(x, sc_m, sh_m, g_m, s5_tables, s5_d, w_glu, b_glu, ln_g[i, 0], ln_b[i, 0])
        else:
            x = _attn_layer(x, sc_m, sh_m, g_m, w_qkv, b_qkv, rpb, w_o, b_o, ln_g[i, 0], ln_b[i, 0])
        x = _ffn_layer(x, sc_f, sh_f, g_f, ffn_w_in[i], ffn_b_in[i], ffn_conv_w[i], ffn_conv_b[i],
                       ffn_w_out[i], ffn_b_out[i], ln_g[i, 1], ln_b[i, 1])
    return x


def kernel(x_prompt, x_sample, c_prompt, c_sample, w_ada, b_ada, ln_g, ln_b, s5_lam_re, s5_lam_im, s5_log_dt, s5_b_re, s5_b_im, s5_c_re, s5_c_im, s5_d, s5_w_glu, s5_b_glu, na_w_qkv, na_b_qkv, na_rpb, na_w_o, na_b_o, ffn_w_in, ffn_b_in, ffn_conv_w, ffn_conv_b, ffn_w_out, ffn_b_out):
    assert _DEPTH == 2 and x_prompt.shape[-1] == _D
    nb_p, nb_s = c_prompt.shape[0], c_sample.shape[0]
    nb_pad = -(-(nb_p + nb_s) // _SUBLANES) * _SUBLANES
    c_all = jnp.zeros((nb_pad, _D), _f32).at[:nb_p].set(c_prompt).at[nb_p:nb_p + nb_s].set(c_sample)
    mod = _modulation(c_all, w_ada, b_ada)

    tables = _s5_slab_tables(s5_lam_re[0], s5_lam_im[0], s5_log_dt[0], s5_b_re[0], s5_b_im[0],
                        s5_c_re[0], s5_c_im[0])
    shared = (tables, s5_d[0], s5_w_glu[0].astype(_bf16), s5_b_glu[0],
              na_w_qkv[0].astype(_bf16), na_b_qkv[0], na_rpb[0], na_w_o[0].astype(_bf16), na_b_o[0],
              ffn_w_in.astype(_bf16), ffn_b_in, ffn_conv_w, ffn_conv_b, ffn_w_out.astype(_bf16), ffn_b_out,
              ln_g, ln_b)
    y_prompt = _trunk(x_prompt, mod[:, :nb_p], *shared)
    y_sample = _trunk(x_sample, mod[:, nb_p:nb_p + nb_s], *shared)
    return (y_prompt, y_sample)
```

```python
import functools

import numpy as np
import jax
import jax.numpy as jnp
from jax import lax
from jax.experimental import pallas as pl
from jax.experimental.pallas import tpu as pltpu

_D = 1024
_DEPTH = 2
_GRID_W = 64
_GROUP_DIM = 16
_N_GROUPS = _D // _GROUP_DIM
_STATE_DIM = 64
_N_HEADS = 16
_HEAD_DIM = _D // _N_HEADS
_WIN_H = 8
_WIN_W = 16
_D_FF = 2816
_ALPHA = (2 * _DEPTH) ** 0.25
_LN_EPS = 1e-5

_LANES = 128
_SUBLANES = 8
_BF16_ROWS = 16
_VMEM_LIMIT = 56 * 1024 * 1024

_L = 16
_GW = _L * _GROUP_DIM
_SG = 4
_SLAB_LANES = _SG * _GROUP_DIM
_N_SLABS = _N_GROUPS // _SG
_N_TILES = _D // _LANES
_CW = _L * _SLAB_LANES
_CS = _SG * _STATE_DIM
_TS = 2 * _CS
_STATE_W = _N_GROUPS * _STATE_DIM
_S5_ROWS = 256
_S5_PARTS = 1
_SCAN_ROWS = 256

_TM = 512
_GLU_PARTS = 2
_FFN_TM = 1024
_FFN_SPLIT = (0, 1024, 2048, _D_FF)
_Q_ROWS = 8
_BAND = _WIN_H * _GRID_W
_ATT_SKEW = 3
_NEG = -1e30

_f32 = jnp.float32
_bf16 = jnp.bfloat16


def _params(sem):
    return pltpu.CompilerParams(dimension_semantics=sem, vmem_limit_bytes=_VMEM_LIMIT)


def _const_spec(shape):
    nd = len(shape)
    return pl.BlockSpec(shape, lambda *_: (0,) * nd, pipeline_mode=pl.Buffered(1))


def _layer_norm(z, g, b):
    mu = jnp.mean(z, axis=-1, keepdims=True)
    d = z - mu
    var = jnp.mean(d * d, axis=-1, keepdims=True)
    return d * lax.rsqrt(var + _LN_EPS) * g + b


def _dot(a, b):
    return jnp.dot(a, b, preferred_element_type=_f32)


def _mod_kernel(c_ref, w_ref, b_ref, o_ref):
    cond = jax.nn.silu(c_ref[...]).astype(_bf16)
    o_ref[...] = _dot(cond, w_ref[...].astype(_bf16)) + b_ref[...]


def _modulation(c_all, w_ada, b_ada):
    nb = c_all.shape[0]
    tn = 1536
    return pl.pallas_call(
        _mod_kernel,
        grid=(_DEPTH, 6 * _D // tn),
        in_specs=[
            pl.BlockSpec((nb, _D), lambda l, n: (0, 0)),
            pl.BlockSpec((None, _D, tn), lambda l, n: (l, 0, n)),
            pl.BlockSpec((None, 1, tn), lambda l, n: (l, 0, n)),
        ],
        out_specs=pl.BlockSpec((None, nb, tn), lambda l, n: (l, 0, n)),
        out_shape=jax.ShapeDtypeStruct((_DEPTH, nb, 6 * _D), _f32),
        compiler_params=_params(("arbitrary", "arbitrary")),
        name="adaln_mod",
    )(c_all, w_ada, b_ada.reshape(_DEPTH, 1, 6 * _D))


def _s5_tables(lam_re, lam_im, log_dt, b_re, b_im, c_re, c_im):
    hi = lax.Precision.HIGHEST
    dt = jnp.exp(log_dt)[..., None]
    z_re, z_im = lam_re * dt, lam_im * dt
    mag = jnp.exp(z_re)
    ab_re, ab_im = mag * jnp.cos(z_im), mag * jnp.sin(z_im)
    den = lam_re * lam_re + lam_im * lam_im
    nr, ni = ab_re - 1.0, ab_im
    f_re = (nr * lam_re + ni * lam_im) / den
    f_im = (ni * lam_re - nr * lam_im) / den
    bb_re = f_re[..., None] * b_re - f_im[..., None] * b_im
    bb_im = f_re[..., None] * b_im + f_im[..., None] * b_re
    bb_re = bb_re.transpose(0, 3, 1, 2).reshape(2, _GROUP_DIM, _STATE_W)
    bb_im = bb_im.transpose(0, 3, 1, 2).reshape(2, _GROUP_DIM, _STATE_W)

    k = jnp.arange(_L + 1, dtype=_f32)[:, None, None]
    zf_re, zf_im = z_re.reshape(2, _STATE_W), z_im.reshape(2, _STATE_W)
    pw_mag = jnp.exp(k * zf_re)
    pw_re, pw_im = pw_mag * jnp.cos(k * zf_im), pw_mag * jnp.sin(k * zf_im)

    abb_re = pw_re[:, :, None, :] * bb_re - pw_im[:, :, None, :] * bb_im
    abb_im = pw_re[:, :, None, :] * bb_im + pw_im[:, :, None, :] * bb_re

    by_group = (_L, 2, _GROUP_DIM, _N_GROUPS, _STATE_DIM)
    lag = (jnp.einsum('kdhgp,dgop->kdgho', abb_re[:_L].reshape(by_group), c_re, precision=hi)
           - jnp.einsum('kdhgp,dgop->kdgho', abb_im[:_L].reshape(by_group), c_im, precision=hi))
    lag_f, lag_b = lag[:, 0], lag[:, 1]
    both = jnp.concatenate([lag_b[:0:-1], (lag_f[0] + lag_b[0])[None], lag_f[1:]], axis=0)
    both = both.reshape(2 * _L - 1, _N_SLABS, _SG, _GROUP_DIM, _GROUP_DIM).transpose(1, 0, 3, 2, 4)
    both = both.reshape(_N_SLABS, 2 * _L - 1, _GROUP_DIM, _SLAB_LANES)
    toep = jnp.concatenate([both[:, :-1], both[:, 1:]], axis=-1)

    rev = np.arange(_L - 1, -1, -1)
    st = jnp.stack([abb_re[rev, 0], abb_im[rev, 0], abb_re[:_L, 1], abb_im[:_L, 1]], axis=0)
    w_state = st.reshape(4, _GW, _STATE_W)

    ct_re = c_re.transpose(0, 2, 1, 3).reshape(2, _GROUP_DIM, _STATE_W)
    ct_im = c_im.transpose(0, 2, 1, 3).reshape(2, _GROUP_DIM, _STATE_W)

    def out_mat(d, powers):
        pr_, pi_ = pw_re[powers, d][:, None, :], pw_im[powers, d][:, None, :]
        wr = ct_re[d] * pr_ - ct_im[d] * pi_
        wi = ct_re[d] * pi_ + ct_im[d] * pr_
        return wr, -wi

    wfr, wfi = out_mat(0, np.arange(1, _L + 1))
    wbr, wbi = out_mat(1, np.arange(_L, 0, -1))
    w_out = jnp.stack([wfr, wfi, wbr, wbi], axis=0).reshape(4, _GW, _STATE_W)

    flow = np.stack([np.arange(_SUBLANES), np.arange(_SUBLANES - 1, -1, -1)])[:, :, None, None]

    def chunk_pow(m):
        e = jnp.asarray(m, _f32) * float(_L)
        pm = jnp.exp(e * z_re[:, None])
        return jnp.stack([pm * jnp.cos(e * z_im[:, None]), pm * jnp.sin(e * z_im[:, None])], axis=1)

    def masked(d):
        return chunk_pow(np.full_like(flow, d)) * jnp.asarray(flow >= d, _f32)[:, None]

    scan_tab = jnp.stack([chunk_pow(np.ones_like(flow)), masked(1), masked(2), masked(4), chunk_pow(flow)], axis=1)
    scan_tab = scan_tab.reshape(2, 5, 2, _SUBLANES, _STATE_W)
    return toep, w_state, w_out, scan_tab


def _expand_toep_kernel(z_ref, o_ref):
    shape = (_SLAB_LANES, _LANES)
    row_group = lax.shift_right_logical(lax.broadcasted_iota(jnp.int32, shape, 0), _GROUP_DIM.bit_length() - 1)
    lane_group = lax.shift_right_logical(lax.broadcasted_iota(jnp.int32, shape, 1) & (_SLAB_LANES - 1),
                                         _GROUP_DIM.bit_length() - 1)
    keep = row_group == lane_group
    for s in range(_L):
        for m in range(_L // 2):
            strip = z_ref[2 * m - s + _L - 1]
            blk = jnp.concatenate([strip] * _SG, axis=0)
            o_ref[s * _SLAB_LANES:(s + 1) * _SLAB_LANES, m * _LANES:(m + 1) * _LANES] = (
                jnp.where(keep, blk, 0.0).astype(_bf16))


def _expand_toep(strips):
    return pl.pallas_call(
        _expand_toep_kernel,
        grid=(_N_SLABS,),
        in_specs=[pl.BlockSpec((None,) + strips.shape[1:], lambda q: (q, 0, 0, 0))],
        out_specs=pl.BlockSpec((None, _CW, _CW), lambda q: (q, 0, 0)),
        out_shape=jax.ShapeDtypeStruct((_N_SLABS, _CW, _CW), _bf16),
        compiler_params=_params(("arbitrary",)),
        name="s5_expand_toep",
    )(strips)


def _group_spread(n, run, g):
    j = lax.broadcasted_iota(jnp.int32, (n, _SG * n), 0)
    c = lax.broadcasted_iota(jnp.int32, (n, _SG * n), 1)
    shift = run.bit_length() - 1
    spread = lax.shift_left(lax.shift_right_logical(j, shift), shift + _SG.bit_length() - 1) + (j & (run - 1))
    return jnp.where(c == spread + g * run, 1.0, 0.0).astype(_bf16)


def _expand_state_kernel(a_ref, o_ref):
    lane_group = lax.shift_right_logical(
        lax.broadcasted_iota(jnp.int32, (_GROUP_DIM, _CS), 1), _STATE_DIM.bit_length() - 1)
    for c in range(4):
        for g in range(_SG):
            for s in range(_L):
                blk = a_ref[c, s * _GROUP_DIM:(s + 1) * _GROUP_DIM, :]
                r0 = (s * _SG + g) * _GROUP_DIM
                o_ref[r0:r0 + _GROUP_DIM, c * _CS:(c + 1) * _CS] = jnp.where(lane_group == g, blk, 0.0).astype(_bf16)


def _expand_out_kernel(a_ref, o_ref):
    for c in range(4):
        xt = a_ref[c].T.astype(_bf16)
        for g in range(_SG):
            rows = slice(g * _STATE_DIM, (g + 1) * _STATE_DIM)
            placed = _dot(xt[rows, :], _group_spread(_GW, _GROUP_DIM, g))
            o_ref[c * _CS + g * _STATE_DIM:c * _CS + (g + 1) * _STATE_DIM, :] = placed.astype(_bf16)


def _expand_states(kern, table):
    return pl.pallas_call(
        kern,
        grid=(_N_SLABS,),
        in_specs=[pl.BlockSpec((4, _GW, _CS), lambda q: (0, 0, q))],
        out_specs=pl.BlockSpec((None, _CW, _CW), lambda q: (q, 0, 0)),
        out_shape=jax.ShapeDtypeStruct((_N_SLABS, _CW, _CW), _bf16),
        compiler_params=_params(("arbitrary",)),
        name="s5_expand_states",
    )(table)


def _s5_slab_tables(*s5_params):
    toep, w_state, w_out, scan_tab = _s5_tables(*s5_params)
    return (_expand_toep(toep), _expand_states(_expand_state_kernel, w_state),
            _expand_states(_expand_out_kernel, w_out), scan_tab)


def _chunk_tokens(x_ref, sc_ref, sh_ref, c0, rows):
    sc1 = 1.0 + sc_ref[...]
    sh = sh_ref[...]
    return [x_ref[pl.ds(c0 * _L + s, rows, stride=_L), :] * sc1 + sh for s in range(_L)]


def _low_half(rows):
    return lax.broadcasted_iota(jnp.int32, (rows, _LANES), 1) < _SLAB_LANES


def _slab_chunks(us):
    low = _low_half(us[0].shape[0])
    lo, hi = [], []
    for m in range(_L // 2):
        a, b = us[2 * m], us[2 * m + 1]
        lo.append(jnp.where(low, a, pltpu.roll(b, _SLAB_LANES, axis=1)))
        hi.append(jnp.where(low, pltpu.roll(a, _SLAB_LANES, axis=1), b))
    return (jnp.concatenate(lo, axis=-1).astype(_bf16), jnp.concatenate(hi, axis=-1).astype(_bf16))


def _row_parts(x_ref):
    rows = x_ref.shape[0] // _L
    part = rows // _S5_PARTS
    return [(p * part, part) for p in range(_S5_PARTS)]


def _s5_state_kernel(x_ref, sc_ref, sh_ref, w_ref, *o_refs):
    for c0, rows in _row_parts(x_ref):
        us = _chunk_tokens(x_ref, sc_ref, sh_ref, c0, rows)
        for h, u in enumerate(_slab_chunks(us)):
            r = _dot(u, w_ref[h])
            for c, o_ref in enumerate(o_refs):
                o_ref[c0:c0 + rows, h * _CS:(h + 1) * _CS] = r[:, c * _CS:(c + 1) * _CS]


def _s5_scan_kernel(lfr, lfi, lbr, lbi, tab_ref, ofr, ofi, obr, obi, carry):
    @pl.when(pl.program_id(2) == 0)
    def _():
        carry[...] = jnp.zeros_like(carry)

    tiles = lfr.shape[0] // _SUBLANES
    sub = lax.broadcasted_iota(jnp.int32, (_SUBLANES, lfr.shape[1]), 0)

    def cmul(ar, ai, br, bi):
        return ar * br - ai * bi, ar * bi + ai * br

    def tile_scan(xr, xi, f0r, f0i, d):
        def shift(v, k):
            return pltpu.roll(v, k if d == 0 else _SUBLANES - k, axis=0)

        ir, ii = xr, xi
        for n, k in ((1, 1), (2, 2), (3, 4)):
            pr, pi = cmul(tab_ref[d, n, 0], tab_ref[d, n, 1], shift(ir, k), shift(ii, k))
            ir, ii = ir + pr, ii + pi
        first = sub == (0 if d == 0 else _SUBLANES - 1)
        cr, ci = cmul(tab_ref[d, 4, 0], tab_ref[d, 4, 1], f0r, f0i)
        fin_r = cr + jnp.where(first, 0.0, shift(ir, 1))
        fin_i = ci + jnp.where(first, 0.0, shift(ii, 1))
        nr, ni = cmul(tab_ref[d, 0, 0], tab_ref[d, 0, 1], fin_r, fin_i)
        nr, ni = nr + xr, ni + xi
        end = _SUBLANES - 1 if d == 0 else 0
        return (fin_r, fin_i, jnp.broadcast_to(nr[end:end + 1], nr.shape),
                jnp.broadcast_to(ni[end:end + 1], ni.shape))

    def body(i, c):
        rf = pl.multiple_of(i * _SUBLANES, _SUBLANES)
        rb = pl.multiple_of((tiles - 1 - i) * _SUBLANES, _SUBLANES)
        fr, fi, c0, c1 = tile_scan(lfr[pl.ds(rf, _SUBLANES), :], lfi[pl.ds(rf, _SUBLANES), :], c[0], c[1], 0)
        ofr[pl.ds(rf, _SUBLANES), :] = fr
        ofi[pl.ds(rf, _SUBLANES), :] = fi
        br, bi, c2, c3 = tile_scan(lbr[pl.ds(rb, _SUBLANES), :], lbi[pl.ds(rb, _SUBLANES), :], c[2], c[3], 1)
        obr[pl.ds(rb, _SUBLANES), :] = br
        obi[pl.ds(rb, _SUBLANES), :] = bi
        return (c0, c1, c2, c3)

    c = lax.fori_loop(0, tiles, body, (carry[0], carry[1], carry[2], carry[3]))
    for q in range(4):
        carry[q] = c[q]


def _s5_out_kernel(x_ref, sc_ref, sh_ref, d_ref, sfr, sfi, sbr, sbi, t_ref, w_ref, o_ref):
    s_refs = (sfr, sfi, sbr, sbi)
    d = d_ref[...]
    for c0, rows in _row_parts(x_ref):
        us = _chunk_tokens(x_ref, sc_ref, sh_ref, c0, rows)
        ys = []
        for h, u in enumerate(_slab_chunks(us)):
            st = jnp.concatenate([s[c0:c0 + rows, h * _CS:(h + 1) * _CS] for s in s_refs], axis=-1)
            ys.append(_dot(u, t_ref[h]) + _dot(st.astype(_bf16), w_ref[h]))
        low = _low_half(rows)
        for t in range(_L):
            cols = slice((t // 2) * _LANES, (t // 2 + 1) * _LANES)
            y_lo, y_hi = ys[0][:, cols], ys[1][:, cols]
            if t % 2 == 0:
                y_t = jnp.where(low, y_lo, pltpu.roll(y_hi, _SLAB_LANES, axis=1))
            else:
                y_t = jnp.where(low, pltpu.roll(y_lo, _SLAB_LANES, axis=1), y_hi)
            o_ref[pl.ds(c0 * _L + t, rows, stride=_L), :] = jax.nn.gelu(d * us[t] + y_t)


def _glu_ln_kernel(y_ref, x_ref, w_ref, b_ref, gate_ref, lng_ref, lnb_ref, o_ref):
    part = y_ref.shape[0] // _GLU_PARTS
    for r in (slice(n * part, (n + 1) * part) for n in range(_GLU_PARTS)):
        ag = _dot(y_ref[r, :].astype(_bf16), w_ref[...]) + b_ref[...]
        mix = ag[:, :_D] * jax.nn.sigmoid(ag[:, _D:])
        z = _ALPHA * x_ref[r, :] + gate_ref[...] * mix
        o_ref[r, :] = _layer_norm(z, lng_ref[...], lnb_ref[...])


def _s5_layer(x, sc, sh, gate, tables, d_skip, w_glu, b_glu, ln_g, ln_b):
    bsz, slen, _ = x.shape
    cs = slen // _L
    nt = bsz * slen
    toep, w_state, w_out, scan_tab = tables
    rows = min(cs, _S5_ROWS)
    bpc = cs // rows
    nblk = bsz * bpc
    x2 = x.reshape(nt, _D)
    sc3, sh3 = sc.reshape(bsz, 1, _D), sh.reshape(bsz, 1, _D)

    x_spec = pl.BlockSpec((rows * _L, _LANES), lambda q, i: (i, q))
    mod_spec = pl.BlockSpec((None, 1, _LANES), lambda q, i: (i // bpc, 0, q))
    st_spec = pl.BlockSpec((None, rows, _TS), lambda q, i: (i // bpc, i % bpc, q))
    st_shape = jax.ShapeDtypeStruct((bsz, cs, _STATE_W), _f32)
    slab_spec = pl.BlockSpec((2, _CW, _CW), lambda q, i: (q, 0, 0))

    loc = pl.pallas_call(
        _s5_state_kernel,
        grid=(_N_TILES, nblk),
        in_specs=[x_spec, mod_spec, mod_spec, slab_spec],
        out_specs=[st_spec] * 4,
        out_shape=[st_shape] * 4,
        compiler_params=_params(("arbitrary", "arbitrary")),
        name="s5_chunk_states",
    )(x2, sc3, sh3, w_state)

    cblk = min(cs, _SCAN_ROWS)
    nsb = cs // cblk
    fwd_spec = pl.BlockSpec((None, cblk, _TS), lambda b, q, j: (b, j, q))
    bwd_spec = pl.BlockSpec((None, cblk, _TS), lambda b, q, j: (b, nsb - 1 - j, q))
    sin = pl.pallas_call(
        _s5_scan_kernel,
        grid=(bsz, _N_TILES, nsb),
        in_specs=[fwd_spec, fwd_spec, bwd_spec, bwd_spec,
                  pl.BlockSpec((2, 5, 2, _SUBLANES, _TS), lambda b, q, j: (0, 0, 0, 0, q))],
        out_specs=[fwd_spec, fwd_spec, bwd_spec, bwd_spec],
        out_shape=[st_shape] * 4,
        scratch_shapes=[pltpu.VMEM((4, _SUBLANES, _TS), _f32)],
        compiler_params=_params(("arbitrary", "arbitrary", "arbitrary")),
        name="s5_chunk_scan",
    )(*loc, scan_tab)

    y = pl.pallas_call(
        _s5_out_kernel,
        grid=(_N_TILES, nblk),
        in_specs=[x_spec, mod_spec, mod_spec, pl.BlockSpec((1, _LANES), lambda q, i: (0, q)),
                  st_spec, st_spec, st_spec, st_spec, slab_spec, slab_spec],
        out_specs=x_spec,
        out_shape=jax.ShapeDtypeStruct((nt, _D), _f32),
        compiler_params=_params(("arbitrary", "arbitrary")),
        name="s5_chunk_out",
    )(x2, sc3, sh3, d_skip.reshape(1, _D), *sin, toep, w_out)

    bps = slen // _TM
    row_spec = pl.BlockSpec((_TM, _D), lambda i: (i, 0))
    seq_spec = pl.BlockSpec((None, 1, _D), lambda i: (i // bps, 0, 0))
    out = pl.pallas_call(
        _glu_ln_kernel,
        grid=(nt // _TM,),
        in_specs=[row_spec, row_spec, _const_spec((_D, 2 * _D)), _const_spec((1, 2 * _D)),
                  seq_spec, _const_spec((1, _D)), _const_spec((1, _D))],
        out_specs=row_spec,
        out_shape=jax.ShapeDtypeStruct((nt, _D), _f32),
        compiler_params=_params(("arbitrary",)),
        name="s5_glu_ln",
    )(y, x2, w_glu, b_glu.reshape(1, 2 * _D), gate.reshape(bsz, 1, _D),
      ln_g.reshape(1, _D), ln_b.reshape(1, _D))
    return out.reshape(bsz, slen, _D)


def _ffn_kernel(xp_ref, x_ref, xn_ref, sc_ref, sh_ref, gate_ref, win_ref, bin_ref, cw_ref, cb_ref,
                wout_ref, bout_ref, lng_ref, lnb_ref, o_ref, *, blocks_per_seq):
    i = pl.program_id(0)
    first = (i % blocks_per_seq) == 0
    last = (i % blocks_per_seq) == blocks_per_seq - 1
    halo = _BF16_ROWS
    tm = x_ref.shape[0]
    rows = tm + 2 * halo

    sc1 = 1.0 + sc_ref[...]
    sh = sh_ref[...]
    x = x_ref[...]
    h = (x * sc1 + sh).astype(_bf16)
    h_prev = (xp_ref[...] * sc1 + sh).astype(_bf16)
    h_next = (xn_ref[...] * sc1 + sh).astype(_bf16)
    h_ext = jnp.concatenate([h_prev, h, h_next], axis=0)

    rid = lax.broadcasted_iota(jnp.int32, (rows, 1), 0)
    pad = (first & (rid == halo - 1)) | (last & (rid == halo + tm))

    splits = list(zip(_FFN_SPLIT[:-1], _FFN_SPLIT[1:]))
    acts = []
    for lo, hi in splits:
        u = _dot(h_ext, win_ref[:, lo:hi]) + bin_ref[:, lo:hi]
        u = jnp.where(pad, 0.0, u)
        g = _dot(h, win_ref[:, _D_FF + lo:_D_FF + hi]) + bin_ref[:, _D_FF + lo:_D_FF + hi]
        up = pltpu.roll(u, 1, axis=0)[halo:halo + tm]
        dn = pltpu.roll(u, rows - 1, axis=0)[halo:halo + tm]
        conv = (up * cw_ref[0:1, lo:hi] + u[halo:halo + tm] * cw_ref[1:2, lo:hi]
                + dn * cw_ref[2:3, lo:hi] + cb_ref[:, lo:hi])
        acts.append((jax.nn.gelu(conv) * g).astype(_bf16))
    half = tm // 2
    for r in (slice(0, half), slice(half, tm)):
        y = bout_ref[...] + sum(_dot(a[r, :], wout_ref[lo:hi, :]) for a, (lo, hi) in zip(acts, splits))
        z = _ALPHA * x[r, :] + gate_ref[...] * y
        o_ref[r, :] = _layer_norm(z, lng_ref[...], lnb_ref[...])


def _ffn_layer(x, sc, sh, gate, layer, w_in, b_in, conv_w, conv_b, w_out, b_out, ln_g, ln_b):
    bsz, slen, _ = x.shape

    def layer_spec(*shape):
        return pl.BlockSpec((None,) + shape, lambda i: (layer,) + (0,) * len(shape), pipeline_mode=pl.Buffered(1))

    nt = bsz * slen
    bps = slen // _FFN_TM
    hb = _FFN_TM // _BF16_ROWS
    n_halo = nt // _BF16_ROWS
    row_spec = pl.BlockSpec((_FFN_TM, _D), lambda i: (i, 0))
    prev_spec = pl.BlockSpec((_BF16_ROWS, _D), lambda i: (jnp.maximum(i * hb - 1, 0), 0))
    next_spec = pl.BlockSpec((_BF16_ROWS, _D), lambda i: (jnp.minimum((i + 1) * hb, n_halo - 1), 0))
    seq_spec = pl.BlockSpec((None, 1, _D), lambda i: (i // bps, 0, 0))
    x2 = x.reshape(nt, _D)
    out = pl.pallas_call(
        functools.partial(_ffn_kernel, blocks_per_seq=bps),
        grid=(nt // _FFN_TM,),
        in_specs=[prev_spec, row_spec, next_spec, seq_spec, seq_spec, seq_spec,
                  layer_spec(_D, 2 * _D_FF), layer_spec(1, 2 * _D_FF),
                  layer_spec(3, _D_FF), layer_spec(1, _D_FF),
                  layer_spec(_D_FF, _D), layer_spec(1, _D),
                  _const_spec((1, _D)), _const_spec((1, _D))],
        out_specs=row_spec,
        out_shape=jax.ShapeDtypeStruct((nt, _D), _f32),
        compiler_params=_params(("arbitrary",)),
        name="conv_ffn_ln",
    )(x2, x2, x2, sc.reshape(bsz, 1, _D), sh.reshape(bsz, 1, _D), gate.reshape(bsz, 1, _D),
      w_in, b_in.reshape(_DEPTH, 1, 2 * _D_FF), conv_w, conv_b.reshape(_DEPTH, 1, _D_FF),
      w_out, b_out.reshape(_DEPTH, 1, _D), ln_g.reshape(1, _D), ln_b.reshape(1, _D))
    return out.reshape(bsz, slen, _D)


def _qkv_kernel(x_ref, sc_ref, sh_ref, w_ref, b_ref, q_ref, k_ref, v_ref):
    h = (x_ref[...] * (1.0 + sc_ref[...]) + sh_ref[...]).astype(_bf16)
    qkv = _dot(h, w_ref[...]) + b_ref[...]
    q_ref[...] = (qkv[:, :_D] * (_HEAD_DIM ** -0.5)).astype(_bf16)
    k_ref[...] = qkv[:, _D:2 * _D].astype(_bf16)
    v_ref[...] = qkv[:, 2 * _D:].astype(_bf16)


def _bias_table(rpb):
    col = np.arange(_GRID_W)
    start = np.clip(col - _WIN_W // 2, 0, _GRID_W - _WIN_W)
    kc = col[None, :]
    inside = (kc >= start[:, None]) & (kc < start[:, None] + _WIN_W)
    off = kc - col[:, None] + (_WIN_W - 1)
    pick = ((off[None] == np.arange(2 * _WIN_W - 1)[:, None, None]) & inside[None]).astype(np.float32)
    t2 = (jnp.einsum('hrc,cqk->hrqk', rpb, pick, precision=lax.Precision.HIGHEST)
          + np.where(inside, 0.0, _NEG).astype(np.float32))
    return jnp.concatenate([t2[:, :-1], t2[:, 1:]], axis=-1)


def _kv_window_start(jb, grid_rows):
    return jnp.clip(_Q_ROWS * jb - _WIN_H // 2, 0, grid_rows - 2 * _Q_ROWS)


def _attn_kernel(q_ref, k_ref, v_ref, t_ref, x_ref, gate_ref,
                 wo_ref, bo_ref, lng_ref, lnb_ref, o_ref, att, *, grid_rows):
    jb = pl.program_id(1)
    w0 = _kv_window_start(jb, grid_rows)

    lane = lax.broadcasted_iota(jnp.int32, (_GRID_W, 2 * _HEAD_DIM), 1)
    low = lane < _HEAD_DIM
    n_pairs = _N_HEADS // 2

    def row_body(i, carry):
        r = _Q_ROWS * jb + i
        rs = jnp.clip(r - _WIN_H // 2, 0, grid_rows - _WIN_H)
        off = pl.multiple_of((rs - w0) * _GRID_W, _GRID_W)
        ro0 = rs - r + (_WIN_H - 1)
        qrow = pl.multiple_of(i * _GRID_W, _GRID_W)
        cols = [slice(hp * 2 * _HEAD_DIM, (hp + 1) * 2 * _HEAD_DIM) for hp in range(n_pairs)]

        def scores(hp):
            q2 = q_ref[pl.ds(qrow, _GRID_W), cols[hp]]
            k2 = k_ref[pl.ds(off, _BAND), cols[hp]]
            zero = jnp.zeros_like(q2)
            qq = jnp.concatenate([jnp.where(low, q2, zero), jnp.where(low, zero, q2)], axis=0)
            return lax.dot_general(qq, k2, (((1,), (1,)), ((), ())), preferred_element_type=_f32)

        def softmax(hp, s):
            bias = jnp.concatenate(
                [jnp.concatenate([t_ref[2 * hp + e, ro0 + 2 * m] for m in range(_WIN_H // 2)], axis=-1)
                 for e in range(2)], axis=0)
            s = s + bias
            p = jnp.exp(s - jnp.max(s, axis=-1, keepdims=True))
            return p.astype(_bf16), jnp.sum(p, axis=-1, keepdims=True)

        def values(hp, ps):
            v2 = v_ref[pl.ds(off, _BAND), cols[hp]]
            o = _dot(ps[0], v2) / ps[1]
            att[pl.ds(qrow, _GRID_W), cols[hp]] = jnp.where(low, o[:_GRID_W], o[_GRID_W:])

        s_q, p_q = {}, {}
        for n in range(n_pairs + 2 * _ATT_SKEW):
            if n < n_pairs:
                s_q[n] = scores(n)
            if 0 <= n - _ATT_SKEW < n_pairs:
                p_q[n - _ATT_SKEW] = softmax(n - _ATT_SKEW, s_q.pop(n - _ATT_SKEW))
            if 0 <= n - 2 * _ATT_SKEW < n_pairs:
                values(n - 2 * _ATT_SKEW, p_q.pop(n - 2 * _ATT_SKEW))
        return carry

    lax.fori_loop(0, _Q_ROWS, row_body, 0)

    y = _dot(att[...].astype(_bf16), wo_ref[...]) + bo_ref[...]
    z = _ALPHA * x_ref[...] + gate_ref[...] * y
    o_ref[...] = _layer_norm(z, lng_ref[...], lnb_ref[...])


def _attn_layer(x, sc, sh, gate, w_qkv, b_qkv, rpb, w_o, b_o, ln_g, ln_b):
    bsz, slen, _ = x.shape
    nt = bsz * slen
    bps = slen // _TM
    x2 = x.reshape(nt, _D)
    row_spec = pl.BlockSpec((_TM, _D), lambda i: (i, 0))
    seq_spec = pl.BlockSpec((None, 1, _D), lambda i: (i // bps, 0, 0))
    qkv_shape = jax.ShapeDtypeStruct((nt, _D), _bf16)
    q, k, v = pl.pallas_call(
        _qkv_kernel,
        grid=(nt // _TM,),
        in_specs=[row_spec, seq_spec, seq_spec, _const_spec((_D, 3 * _D)), _const_spec((1, 3 * _D))],
        out_specs=[row_spec] * 3,
        out_shape=[qkv_shape] * 3,
        compiler_params=_params(("arbitrary",)),
        name="attn_qkv",
    )(x2, sc.reshape(bsz, 1, _D), sh.reshape(bsz, 1, _D), w_qkv, b_qkv.reshape(1, 3 * _D))

    grid_rows = slen // _GRID_W
    nqb = grid_rows // _Q_ROWS
    blk = _Q_ROWS * _GRID_W
    cur = pl.BlockSpec((blk, _D), lambda b, j: (b * nqb + j, 0))
    window = pl.BlockSpec((pl.Element(2 * blk), pl.Element(_D)),
                          lambda b, j: (pl.multiple_of((b * grid_rows + _kv_window_start(j, grid_rows)) * _GRID_W,
                                                       _GRID_W), 0))
    bseq = pl.BlockSpec((None, 1, _D), lambda b, j: (b, 0, 0))
    table = _bias_table(rpb)
    out = pl.pallas_call(
        functools.partial(_attn_kernel, grid_rows=grid_rows),
        grid=(bsz, nqb),
        in_specs=[cur, window, window, _const_spec(table.shape), cur, bseq,
                  _const_spec((_D, _D)), _const_spec((1, _D)), _const_spec((1, _D)), _const_spec((1, _D))],
        out_specs=cur,
        out_shape=jax.ShapeDtypeStruct((nt, _D), _f32),
        scratch_shapes=[pltpu.VMEM((blk, _D), _f32)],
        compiler_params=_params(("arbitrary", "arbitrary")),
        name="nbr_attn_ln",
    )(q, k, v, table, x2, gate.reshape(bsz, 1, _D), w_o, b_o.reshape(1, _D),
      ln_g.reshape(1, _D), ln_b.reshape(1, _D))
    return out.reshape(bsz, slen, _D)


def _trunk(x, mod, s5_tables, s5_d, w_glu, b_glu, w_qkv, b_qkv, rpb, w_o, b_o,
           ffn_w_in, ffn_b_in, ffn_conv_w, ffn_conv_b, ffn_w_out, ffn_b_out, ln_g, ln_b):
    for i in range(_DEPTH):
        sh_m, sc_m, g_m, sh_f, sc_f, g_f = [mod[i, :, n * _D:(n + 1) * _D] for n in range(6)]
        if i % 2 == 0:
            x = _s5_layer(x, sc_m, sh_m, g_m, s5_tables, s5_d, w_glu, b_glu, ln_g[i, 0], ln_b[i, 0])
        else:
            x = _attn_layer(x, sc_m, sh_m, g_m, w_qkv, b_qkv, rpb, w_o, b_o, ln_g[i, 0], ln_b[i, 0])
        x = _ffn_layer(x, sc_f, sh_f, g_f, i, ffn_w_in, ffn_b_in, ffn_conv_w, ffn_conv_b,
                       ffn_w_out, ffn_b_out, ln_g[i, 1], ln_b[i, 1])
    return x


def kernel(x_prompt, x_sample, c_prompt, c_sample, w_ada, b_ada, ln_g, ln_b, s5_lam_re, s5_lam_im, s5_log_dt, s5_b_re, s5_b_im, s5_c_re, s5_c_im, s5_d, s5_w_glu, s5_b_glu, na_w_qkv, na_b_qkv, na_rpb, na_w_o, na_b_o, ffn_w_in, ffn_b_in, ffn_conv_w, ffn_conv_b, ffn_w_out, ffn_b_out):
    assert _DEPTH == 2 and x_prompt.shape[-1] == _D
    nb_p, nb_s = c_prompt.shape[0], c_sample.shape[0]
    nb_pad = -(-(nb_p + nb_s) // _SUBLANES) * _SUBLANES
    c_all = jnp.zeros((nb_pad, _D), _f32).at[:nb_p].set(c_prompt).at[nb_p:nb_p + nb_s].set(c_sample)
    mod = _modulation(c_all, w_ada, b_ada)

    tables = _s5_slab_tables(s5_lam_re[0], s5_lam_im[0], s5_log_dt[0], s5_b_re[0], s5_b_im[0],
                        s5_c_re[0], s5_c_im[0])
    shared = (tables, s5_d[0], s5_w_glu[0].astype(_bf16), s5_b_glu[0],
              na_w_qkv[0].astype(_bf16), na_b_qkv[0], na_rpb[0], na_w_o[0].astype(_bf16), na_b_o[0],
              ffn_w_in.astype(_bf16), ffn_b_in, ffn_conv_w, ffn_conv_b, ffn_w_out.astype(_bf16), ffn_b_out,
              ln_g, ln_b)
    y_prompt = _trunk(x_prompt, mod[:, :nb_p], *shared)
    y_sample = _trunk(x_sample, mod[:, nb_p:nb_p + nb_s], *shared)
    return (y_prompt, y_sample)
```

```python
import functools

import numpy as np
import jax
import jax.numpy as jnp
from jax import lax
from jax.experimental import pallas as pl
from jax.experimental.pallas import tpu as pltpu

_D = 1024
_DEPTH = 2
_GRID_W = 64
_GROUP_DIM = 16
_N_GROUPS = _D // _GROUP_DIM
_STATE_DIM = 64
_N_HEADS = 16
_HEAD_DIM = _D // _N_HEADS
_WIN_H = 8
_WIN_W = 16
_D_FF = 2816
_ALPHA = (2 * _DEPTH) ** 0.25
_LN_EPS = 1e-5

_LANES = 128
_SUBLANES = 8
_BF16_ROWS = 16
_VMEM_LIMIT = 56 * 1024 * 1024

_L = 16
_GW = _L * _GROUP_DIM
_SG = 4
_SLAB_LANES = _SG * _GROUP_DIM
_N_SLABS = _N_GROUPS // _SG
_N_TILES = _D // _LANES
_CW = _L * _SLAB_LANES
_CS = _SG * _STATE_DIM
_TS = 2 * _CS
_STATE_W = _N_GROUPS * _STATE_DIM
_S5_ROWS = 256
_S5_PARTS = 1
_SCAN_ROWS = 256

_TM = 1024
_GLU_PARTS = 2
_FFN_TM = 1024
_FFN_SPLIT = (0, 1024, 2048, _D_FF)
_Q_ROWS = 16
_KV_ROWS = _Q_ROWS + _WIN_H
_BAND = _WIN_H * _GRID_W
_ATT_SKEW = 3
_NEG = -1e30

_f32 = jnp.float32
_bf16 = jnp.bfloat16


def _params(sem):
    return pltpu.CompilerParams(dimension_semantics=sem, vmem_limit_bytes=_VMEM_LIMIT)


def _const_spec(shape):
    nd = len(shape)
    return pl.BlockSpec(shape, lambda *_: (0,) * nd, pipeline_mode=pl.Buffered(1))


def _layer_norm(z, g, b):
    mu = jnp.mean(z, axis=-1, keepdims=True)
    d = z - mu
    var = jnp.mean(d * d, axis=-1, keepdims=True)
    return d * lax.rsqrt(var + _LN_EPS) * g + b


def _dot(a, b):
    return jnp.dot(a, b, preferred_element_type=_f32)


def _mod_kernel(c_ref, w_ref, b_ref, o_ref):
    cond = jax.nn.silu(c_ref[...]).astype(_bf16)
    o_ref[...] = _dot(cond, w_ref[...].astype(_bf16)) + b_ref[...]


def _modulation(c_all, w_ada, b_ada):
    nb = c_all.shape[0]
    tn = 1536
    return pl.pallas_call(
        _mod_kernel,
        grid=(_DEPTH, 6 * _D // tn),
        in_specs=[
            pl.BlockSpec((nb, _D), lambda l, n: (0, 0)),
            pl.BlockSpec((None, _D, tn), lambda l, n: (l, 0, n)),
            pl.BlockSpec((None, 1, tn), lambda l, n: (l, 0, n)),
        ],
        out_specs=pl.BlockSpec((None, nb, tn), lambda l, n: (l, 0, n)),
        out_shape=jax.ShapeDtypeStruct((_DEPTH, nb, 6 * _D), _f32),
        compiler_params=_params(("arbitrary", "arbitrary")),
        name="adaln_mod",
    )(c_all, w_ada, b_ada.reshape(_DEPTH, 1, 6 * _D))


def _s5_tables(lam_re, lam_im, log_dt, b_re, b_im, c_re, c_im):
    hi = lax.Precision.HIGHEST
    dt = jnp.exp(log_dt)[..., None]
    z_re, z_im = lam_re * dt, lam_im * dt
    mag = jnp.exp(z_re)
    ab_re, ab_im = mag * jnp.cos(z_im), mag * jnp.sin(z_im)
    den = lam_re * lam_re + lam_im * lam_im
    nr, ni = ab_re - 1.0, ab_im
    f_re = (nr * lam_re + ni * lam_im) / den
    f_im = (ni * lam_re - nr * lam_im) / den
    bb_re = f_re[..., None] * b_re - f_im[..., None] * b_im
    bb_im = f_re[..., None] * b_im + f_im[..., None] * b_re
    bb_re = bb_re.transpose(0, 3, 1, 2).reshape(2, _GROUP_DIM, _STATE_W)
    bb_im = bb_im.transpose(0, 3, 1, 2).reshape(2, _GROUP_DIM, _STATE_W)

    k = jnp.arange(_L + 1, dtype=_f32)[:, None, None]
    zf_re, zf_im = z_re.reshape(2, _STATE_W), z_im.reshape(2, _STATE_W)
    pw_mag = jnp.exp(k * zf_re)
    pw_re, pw_im = pw_mag * jnp.cos(k * zf_im), pw_mag * jnp.sin(k * zf_im)

    abb_re = pw_re[:, :, None, :] * bb_re - pw_im[:, :, None, :] * bb_im
    abb_im = pw_re[:, :, None, :] * bb_im + pw_im[:, :, None, :] * bb_re

    by_group = (_L, 2, _GROUP_DIM, _N_GROUPS, _STATE_DIM)
    lag = (jnp.einsum('kdhgp,dgop->kdgho', abb_re[:_L].reshape(by_group), c_re, precision=hi)
           - jnp.einsum('kdhgp,dgop->kdgho', abb_im[:_L].reshape(by_group), c_im, precision=hi))
    lag_f, lag_b = lag[:, 0], lag[:, 1]
    both = jnp.concatenate([lag_b[:0:-1], (lag_f[0] + lag_b[0])[None], lag_f[1:]], axis=0)
    both = both.reshape(2 * _L - 1, _N_SLABS, _SG, _GROUP_DIM, _GROUP_DIM).transpose(1, 0, 3, 2, 4)
    both = both.reshape(_N_SLABS, 2 * _L - 1, _GROUP_DIM, _SLAB_LANES)
    toep = jnp.concatenate([both[:, :-1], both[:, 1:]], axis=-1)

    rev = np.arange(_L - 1, -1, -1)
    st = jnp.stack([abb_re[rev, 0], abb_im[rev, 0], abb_re[:_L, 1], abb_im[:_L, 1]], axis=0)
    w_state = st.reshape(4, _GW, _STATE_W)

    ct_re = c_re.transpose(0, 2, 1, 3).reshape(2, _GROUP_DIM, _STATE_W)
    ct_im = c_im.transpose(0, 2, 1, 3).reshape(2, _GROUP_DIM, _STATE_W)

    def out_mat(d, powers):
        pr_, pi_ = pw_re[powers, d][:, None, :], pw_im[powers, d][:, None, :]
        wr = ct_re[d] * pr_ - ct_im[d] * pi_
        wi = ct_re[d] * pi_ + ct_im[d] * pr_
        return wr, -wi

    wfr, wfi = out_mat(0, np.arange(1, _L + 1))
    wbr, wbi = out_mat(1, np.arange(_L, 0, -1))
    w_out = jnp.stack([wfr, wfi, wbr, wbi], axis=0).reshape(4, _GW, _STATE_W)

    flow = np.stack([np.arange(_SUBLANES), np.arange(_SUBLANES - 1, -1, -1)])[:, :, None, None]

    def chunk_pow(m):
        e = jnp.asarray(m, _f32) * float(_L)
        pm = jnp.exp(e * z_re[:, None])
        return jnp.stack([pm * jnp.cos(e * z_im[:, None]), pm * jnp.sin(e * z_im[:, None])], axis=1)

    def masked(d):
        return chunk_pow(np.full_like(flow, d)) * jnp.asarray(flow >= d, _f32)[:, None]

    scan_tab = jnp.stack([chunk_pow(np.ones_like(flow)), masked(1), masked(2), masked(4), chunk_pow(flow)], axis=1)
    scan_tab = scan_tab.reshape(2, 5, 2, _SUBLANES, _STATE_W)
    return toep, w_state, w_out, scan_tab


def _expand_toep_kernel(z_ref, o_ref):
    shape = (_SLAB_LANES, _LANES)
    row_group = lax.shift_right_logical(lax.broadcasted_iota(jnp.int32, shape, 0), _GROUP_DIM.bit_length() - 1)
    lane_group = lax.shift_right_logical(lax.broadcasted_iota(jnp.int32, shape, 1) & (_SLAB_LANES - 1),
                                         _GROUP_DIM.bit_length() - 1)
    keep = row_group == lane_group
    for s in range(_L):
        for m in range(_L // 2):
            strip = z_ref[2 * m - s + _L - 1]
            blk = jnp.concatenate([strip] * _SG, axis=0)
            o_ref[s * _SLAB_LANES:(s + 1) * _SLAB_LANES, m * _LANES:(m + 1) * _LANES] = (
                jnp.where(keep, blk, 0.0).astype(_bf16))


def _expand_toep(strips):
    return pl.pallas_call(
        _expand_toep_kernel,
        grid=(_N_SLABS,),
        in_specs=[pl.BlockSpec((None,) + strips.shape[1:], lambda q: (q, 0, 0, 0))],
        out_specs=pl.BlockSpec((None, _CW, _CW), lambda q: (q, 0, 0)),
        out_shape=jax.ShapeDtypeStruct((_N_SLABS, _CW, _CW), _bf16),
        compiler_params=_params(("arbitrary",)),
        name="s5_expand_toep",
    )(strips)


def _group_spread(n, run, g):
    j = lax.broadcasted_iota(jnp.int32, (n, _SG * n), 0)
    c = lax.broadcasted_iota(jnp.int32, (n, _SG * n), 1)
    shift = run.bit_length() - 1
    spread = lax.shift_left(lax.shift_right_logical(j, shift), shift + _SG.bit_length() - 1) + (j & (run - 1))
    return jnp.where(c == spread + g * run, 1.0, 0.0).astype(_bf16)


def _expand_state_kernel(a_ref, o_ref):
    lane_group = lax.shift_right_logical(
        lax.broadcasted_iota(jnp.int32, (_GROUP_DIM, _CS), 1), _STATE_DIM.bit_length() - 1)
    for c in range(4):
        for g in range(_SG):
            for s in range(_L):
                blk = a_ref[c, s * _GROUP_DIM:(s + 1) * _GROUP_DIM, :]
                r0 = (s * _SG + g) * _GROUP_DIM
                o_ref[r0:r0 + _GROUP_DIM, c * _CS:(c + 1) * _CS] = jnp.where(lane_group == g, blk, 0.0).astype(_bf16)


def _expand_out_kernel(a_ref, o_ref):
    for c in range(4):
        xt = a_ref[c].T.astype(_bf16)
        for g in range(_SG):
            rows = slice(g * _STATE_DIM, (g + 1) * _STATE_DIM)
            placed = _dot(xt[rows, :], _group_spread(_GW, _GROUP_DIM, g))
            o_ref[c * _CS + g * _STATE_DIM:c * _CS + (g + 1) * _STATE_DIM, :] = placed.astype(_bf16)


def _expand_states(kern, table):
    return pl.pallas_call(
        kern,
        grid=(_N_SLABS,),
        in_specs=[pl.BlockSpec((4, _GW, _CS), lambda q: (0, 0, q))],
        out_specs=pl.BlockSpec((None, _CW, _CW), lambda q: (q, 0, 0)),
        out_shape=jax.ShapeDtypeStruct((_N_SLABS, _CW, _CW), _bf16),
        compiler_params=_params(("arbitrary",)),
        name="s5_expand_states",
    )(table)


def _s5_slab_tables(*s5_params):
    toep, w_state, w_out, scan_tab = _s5_tables(*s5_params)
    return (_expand_toep(toep), _expand_states(_expand_state_kernel, w_state),
            _expand_states(_expand_out_kernel, w_out), scan_tab)


def _chunk_tokens(x_ref, sc_ref, sh_ref, c0, rows):
    sc1 = 1.0 + sc_ref[...]
    sh = sh_ref[...]
    return [x_ref[pl.ds(c0 * _L + s, rows, stride=_L), :] * sc1 + sh for s in range(_L)]


def _low_half(rows):
    return lax.broadcasted_iota(jnp.int32, (rows, _LANES), 1) < _SLAB_LANES


def _slab_chunks(us):
    low = _low_half(us[0].shape[0])
    lo, hi = [], []
    for m in range(_L // 2):
        a, b = us[2 * m], us[2 * m + 1]
        lo.append(jnp.where(low, a, pltpu.roll(b, _SLAB_LANES, axis=1)))
        hi.append(jnp.where(low, pltpu.roll(a, _SLAB_LANES, axis=1), b))
    return (jnp.concatenate(lo, axis=-1).astype(_bf16), jnp.concatenate(hi, axis=-1).astype(_bf16))


def _row_parts(x_ref):
    rows = x_ref.shape[0] // _L
    part = rows // _S5_PARTS
    return [(p * part, part) for p in range(_S5_PARTS)]


def _s5_state_kernel(x_ref, sc_ref, sh_ref, w_ref, *o_refs):
    for c0, rows in _row_parts(x_ref):
        us = _chunk_tokens(x_ref, sc_ref, sh_ref, c0, rows)
        for h, u in enumerate(_slab_chunks(us)):
            r = _dot(u, w_ref[h])
            for c, o_ref in enumerate(o_refs):
                o_ref[c0:c0 + rows, h * _CS:(h + 1) * _CS] = r[:, c * _CS:(c + 1) * _CS]


def _s5_scan_kernel(lfr, lfi, lbr, lbi, tab_ref, ofr, ofi, obr, obi, carry):
    @pl.when(pl.program_id(2) == 0)
    def _():
        carry[...] = jnp.zeros_like(carry)

    tiles = lfr.shape[0] // _SUBLANES
    sub = lax.broadcasted_iota(jnp.int32, (_SUBLANES, lfr.shape[1]), 0)

    def cmul(ar, ai, br, bi):
        return ar * br - ai * bi, ar * bi + ai * br

    def tile_scan(xr, xi, f0r, f0i, d):
        def shift(v, k):
            return pltpu.roll(v, k if d == 0 else _SUBLANES - k, axis=0)

        ir, ii = xr, xi
        for n, k in ((1, 1), (2, 2), (3, 4)):
            pr, pi = cmul(tab_ref[d, n, 0], tab_ref[d, n, 1], shift(ir, k), shift(ii, k))
            ir, ii = ir + pr, ii + pi
        first = sub == (0 if d == 0 else _SUBLANES - 1)
        cr, ci = cmul(tab_ref[d, 4, 0], tab_ref[d, 4, 1], f0r, f0i)
        fin_r = cr + jnp.where(first, 0.0, shift(ir, 1))
        fin_i = ci + jnp.where(first, 0.0, shift(ii, 1))
        nr, ni = cmul(tab_ref[d, 0, 0], tab_ref[d, 0, 1], fin_r, fin_i)
        nr, ni = nr + xr, ni + xi
        end = _SUBLANES - 1 if d == 0 else 0
        return (fin_r, fin_i, jnp.broadcast_to(nr[end:end + 1], nr.shape),
                jnp.broadcast_to(ni[end:end + 1], ni.shape))

    def body(i, c):
        rf = pl.multiple_of(i * _SUBLANES, _SUBLANES)
        rb = pl.multiple_of((tiles - 1 - i) * _SUBLANES, _SUBLANES)
        fr, fi, c0, c1 = tile_scan(lfr[pl.ds(rf, _SUBLANES), :], lfi[pl.ds(rf, _SUBLANES), :], c[0], c[1], 0)
        ofr[pl.ds(rf, _SUBLANES), :] = fr
        ofi[pl.ds(rf, _SUBLANES), :] = fi
        br, bi, c2, c3 = tile_scan(lbr[pl.ds(rb, _SUBLANES), :], lbi[pl.ds(rb, _SUBLANES), :], c[2], c[3], 1)
        obr[pl.ds(rb, _SUBLANES), :] = br
        obi[pl.ds(rb, _SUBLANES), :] = bi
        return (c0, c1, c2, c3)

    c = lax.fori_loop(0, tiles, body, (carry[0], carry[1], carry[2], carry[3]))
    for q in range(4):
        carry[q] = c[q]


def _s5_out_kernel(x_ref, sc_ref, sh_ref, d_ref, sfr, sfi, sbr, sbi, t_ref, w_ref, o_ref):
    s_refs = (sfr, sfi, sbr, sbi)
    d = d_ref[...]
    for c0, rows in _row_parts(x_ref):
        us = _chunk_tokens(x_ref, sc_ref, sh_ref, c0, rows)
        ys = []
        for h, u in enumerate(_slab_chunks(us)):
            st = jnp.concatenate([s[c0:c0 + rows, h * _CS:(h + 1) * _CS] for s in s_refs], axis=-1)
            ys.append(_dot(u, t_ref[h]) + _dot(st.astype(_bf16), w_ref[h]))
        low = _low_half(rows)
        for t in range(_L):
            cols = slice((t // 2) * _LANES, (t // 2 + 1) * _LANES)
            y_lo, y_hi = ys[0][:, cols], ys[1][:, cols]
            if t % 2 == 0:
                y_t = jnp.where(low, y_lo, pltpu.roll(y_hi, _SLAB_LANES, axis=1))
            else:
                y_t = jnp.where(low, pltpu.roll(y_lo, _SLAB_LANES, axis=1), y_hi)
            o_ref[pl.ds(c0 * _L + t, rows, stride=_L), :] = jax.nn.gelu(d * us[t] + y_t)


def _glu_ln_kernel(y_ref, x_ref, w_ref, b_ref, gate_ref, lng_ref, lnb_ref, o_ref):
    part = y_ref.shape[0] // _GLU_PARTS
    for r in (slice(n * part, (n + 1) * part) for n in range(_GLU_PARTS)):
        ag = _dot(y_ref[r, :].astype(_bf16), w_ref[...]) + b_ref[...]
        mix = ag[:, :_D] * jax.nn.sigmoid(ag[:, _D:])
        z = _ALPHA * x_ref[r, :] + gate_ref[...] * mix
        o_ref[r, :] = _layer_norm(z, lng_ref[...], lnb_ref[...])


def _s5_layer(x, sc, sh, gate, tables, d_skip, w_glu, b_glu, ln_g, ln_b):
    bsz, slen, _ = x.shape
    cs = slen // _L
    nt = bsz * slen
    toep, w_state, w_out, scan_tab = tables
    rows = min(cs, _S5_ROWS)
    bpc = cs // rows
    nblk = bsz * bpc
    x2 = x.reshape(nt, _D)
    sc3, sh3 = sc.reshape(bsz, 1, _D), sh.reshape(bsz, 1, _D)

    x_spec = pl.BlockSpec((rows * _L, _LANES), lambda q, i: (i, q))
    mod_spec = pl.BlockSpec((None, 1, _LANES), lambda q, i: (i // bpc, 0, q))
    st_spec = pl.BlockSpec((None, rows, _TS), lambda q, i: (i // bpc, i % bpc, q))
    st_shape = jax.ShapeDtypeStruct((bsz, cs, _STATE_W), _f32)
    slab_spec = pl.BlockSpec((2, _CW, _CW), lambda q, i: (q, 0, 0))

    loc = pl.pallas_call(
        _s5_state_kernel,
        grid=(_N_TILES, nblk),
        in_specs=[x_spec, mod_spec, mod_spec, slab_spec],
        out_specs=[st_spec] * 4,
        out_shape=[st_shape] * 4,
        compiler_params=_params(("arbitrary", "arbitrary")),
        name="s5_chunk_states",
    )(x2, sc3, sh3, w_state)

    cblk = min(cs, _SCAN_ROWS)
    nsb = cs // cblk
    fwd_spec = pl.BlockSpec((None, cblk, _TS), lambda b, q, j: (b, j, q))
    bwd_spec = pl.BlockSpec((None, cblk, _TS), lambda b, q, j: (b, nsb - 1 - j, q))
    sin = pl.pallas_call(
        _s5_scan_kernel,
        grid=(bsz, _N_TILES, nsb),
        in_specs=[fwd_spec, fwd_spec, bwd_spec, bwd_spec,
                  pl.BlockSpec((2, 5, 2, _SUBLANES, _TS), lambda b, q, j: (0, 0, 0, 0, q))],
        out_specs=[fwd_spec, fwd_spec, bwd_spec, bwd_spec],
        out_shape=[st_shape] * 4,
        scratch_shapes=[pltpu.VMEM((4, _SUBLANES, _TS), _f32)],
        compiler_params=_params(("arbitrary", "arbitrary", "arbitrary")),
        name="s5_chunk_scan",
    )(*loc, scan_tab)

    y = pl.pallas_call(
        _s5_out_kernel,
        grid=(_N_TILES, nblk),
        in_specs=[x_spec, mod_spec, mod_spec, pl.BlockSpec((1, _LANES), lambda q, i: (0, q)),
                  st_spec, st_spec, st_spec, st_spec, slab_spec, slab_spec],
        out_specs=x_spec,
        out_shape=jax.ShapeDtypeStruct((nt, _D), _f32),
        compiler_params=_params(("arbitrary", "arbitrary")),
        name="s5_chunk_out",
    )(x2, sc3, sh3, d_skip.reshape(1, _D), *sin, toep, w_out)

    bps = slen // _TM
    row_spec = pl.BlockSpec((_TM, _D), lambda i: (i, 0))
    seq_spec = pl.BlockSpec((None, 1, _D), lambda i: (i // bps, 0, 0))
    out = pl.pallas_call(
        _glu_ln_kernel,
        grid=(nt // _TM,),
        in_specs=[row_spec, row_spec, _const_spec((_D, 2 * _D)), _const_spec((1, 2 * _D)),
                  seq_spec, _const_spec((1, _D)), _const_spec((1, _D))],
        out_specs=row_spec,
        out_shape=jax.ShapeDtypeStruct((nt, _D), _f32),
        compiler_params=_params(("arbitrary",)),
        name="s5_glu_ln",
    )(y, x2, w_glu, b_glu.reshape(1, 2 * _D), gate.reshape(bsz, 1, _D),
      ln_g.reshape(1, _D), ln_b.reshape(1, _D))
    return out.reshape(bsz, slen, _D)


def _ffn_kernel(xp_ref, x_ref, xn_ref, sc_ref, sh_ref, gate_ref, win_ref, bin_ref, cw_ref, cb_ref,
                wout_ref, bout_ref, lng_ref, lnb_ref, o_ref, *, blocks_per_seq):
    i = pl.program_id(0)
    first = (i % blocks_per_seq) == 0
    last = (i % blocks_per_seq) == blocks_per_seq - 1
    halo = _BF16_ROWS
    tm = x_ref.shape[0]
    rows = tm + 2 * halo

    sc1 = 1.0 + sc_ref[...]
    sh = sh_ref[...]
    x = x_ref[...]
    h = (x * sc1 + sh).astype(_bf16)
    h_prev = (xp_ref[...] * sc1 + sh).astype(_bf16)
    h_next = (xn_ref[...] * sc1 + sh).astype(_bf16)
    h_ext = jnp.concatenate([h_prev, h, h_next], axis=0)

    rid = lax.broadcasted_iota(jnp.int32, (rows, 1), 0)
    pad = (first & (rid == halo - 1)) | (last & (rid == halo + tm))

    splits = list(zip(_FFN_SPLIT[:-1], _FFN_SPLIT[1:]))
    acts = []
    for lo, hi in splits:
        u = _dot(h_ext, win_ref[:, lo:hi]) + bin_ref[:, lo:hi]
        u = jnp.where(pad, 0.0, u)
        g = _dot(h, win_ref[:, _D_FF + lo:_D_FF + hi]) + bin_ref[:, _D_FF + lo:_D_FF + hi]
        up = pltpu.roll(u, 1, axis=0)[halo:halo + tm]
        dn = pltpu.roll(u, rows - 1, axis=0)[halo:halo + tm]
        conv = (up * cw_ref[0:1, lo:hi] + u[halo:halo + tm] * cw_ref[1:2, lo:hi]
                + dn * cw_ref[2:3, lo:hi] + cb_ref[:, lo:hi])
        acts.append((jax.nn.gelu(conv) * g).astype(_bf16))
    half = tm // 2
    for r in (slice(0, half), slice(half, tm)):
        y = bout_ref[...] + sum(_dot(a[r, :], wout_ref[lo:hi, :]) for a, (lo, hi) in zip(acts, splits))
        z = _ALPHA * x[r, :] + gate_ref[...] * y
        o_ref[r, :] = _layer_norm(z, lng_ref[...], lnb_ref[...])


def _ffn_layer(x, sc, sh, gate, layer, w_in, b_in, conv_w, conv_b, w_out, b_out, ln_g, ln_b):
    bsz, slen, _ = x.shape

    def layer_spec(*shape):
        return pl.BlockSpec((None,) + shape, lambda i: (layer,) + (0,) * len(shape), pipeline_mode=pl.Buffered(1))

    nt = bsz * slen
    bps = slen // _FFN_TM
    hb = _FFN_TM // _BF16_ROWS
    n_halo = nt // _BF16_ROWS
    row_spec = pl.BlockSpec((_FFN_TM, _D), lambda i: (i, 0))
    prev_spec = pl.BlockSpec((_BF16_ROWS, _D), lambda i: (jnp.maximum(i * hb - 1, 0), 0))
    next_spec = pl.BlockSpec((_BF16_ROWS, _D), lambda i: (jnp.minimum((i + 1) * hb, n_halo - 1), 0))
    seq_spec = pl.BlockSpec((None, 1, _D), lambda i: (i // bps, 0, 0))
    x2 = x.reshape(nt, _D)
    out = pl.pallas_call(
        functools.partial(_ffn_kernel, blocks_per_seq=bps),
        grid=(nt // _FFN_TM,),
        in_specs=[prev_spec, row_spec, next_spec, seq_spec, seq_spec, seq_spec,
                  layer_spec(_D, 2 * _D_FF), layer_spec(1, 2 * _D_FF),
                  layer_spec(3, _D_FF), layer_spec(1, _D_FF),
                  layer_spec(_D_FF, _D), layer_spec(1, _D),
                  _const_spec((1, _D)), _const_spec((1, _D))],
        out_specs=row_spec,
        out_shape=jax.ShapeDtypeStruct((nt, _D), _f32),
        compiler_params=_params(("arbitrary",)),
        name="conv_ffn_ln",
    )(x2, x2, x2, sc.reshape(bsz, 1, _D), sh.reshape(bsz, 1, _D), gate.reshape(bsz, 1, _D),
      w_in, b_in.reshape(_DEPTH, 1, 2 * _D_FF), conv_w, conv_b.reshape(_DEPTH, 1, _D_FF),
      w_out, b_out.reshape(_DEPTH, 1, _D), ln_g.reshape(1, _D), ln_b.reshape(1, _D))
    return out.reshape(bsz, slen, _D)


def _qkv_kernel(x_ref, sc_ref, sh_ref, w_ref, b_ref, q_ref, k_ref, v_ref):
    h = (x_ref[...] * (1.0 + sc_ref[...]) + sh_ref[...]).astype(_bf16)
    qkv = _dot(h, w_ref[...]) + b_ref[...]
    q_ref[...] = (qkv[:, :_D] * (_HEAD_DIM ** -0.5)).astype(_bf16)
    k_ref[...] = qkv[:, _D:2 * _D].astype(_bf16)
    v_ref[...] = qkv[:, 2 * _D:].astype(_bf16)


def _bias_table(rpb):
    col = np.arange(_GRID_W)
    start = np.clip(col - _WIN_W // 2, 0, _GRID_W - _WIN_W)
    kc = col[None, :]
    inside = (kc >= start[:, None]) & (kc < start[:, None] + _WIN_W)
    off = kc - col[:, None] + (_WIN_W - 1)
    pick = ((off[None] == np.arange(2 * _WIN_W - 1)[:, None, None]) & inside[None]).astype(np.float32)
    t2 = (jnp.einsum('hrc,cqk->hrqk', rpb, pick, precision=lax.Precision.HIGHEST)
          + np.where(inside, 0.0, _NEG).astype(np.float32))
    return jnp.concatenate([t2[:, :-1], t2[:, 1:]], axis=-1)


def _kv_window_start(jb, grid_rows):
    return jnp.clip(_Q_ROWS * jb - _WIN_H // 2, 0, grid_rows - _KV_ROWS)


def _attn_kernel(q_ref, k_ref, v_ref, t_ref, x_ref, gate_ref,
                 wo_ref, bo_ref, lng_ref, lnb_ref, o_ref, att, *, grid_rows):
    jb = pl.program_id(1)
    w0 = _kv_window_start(jb, grid_rows)

    lane = lax.broadcasted_iota(jnp.int32, (_GRID_W, 2 * _HEAD_DIM), 1)
    low = lane < _HEAD_DIM
    n_pairs = _N_HEADS // 2

    def row_body(i, carry):
        r = _Q_ROWS * jb + i
        rs = jnp.clip(r - _WIN_H // 2, 0, grid_rows - _WIN_H)
        off = pl.multiple_of((rs - w0) * _GRID_W, _GRID_W)
        ro0 = rs - r + (_WIN_H - 1)
        qrow = pl.multiple_of(i * _GRID_W, _GRID_W)
        cols = [slice(hp * 2 * _HEAD_DIM, (hp + 1) * 2 * _HEAD_DIM) for hp in range(n_pairs)]

        def scores(hp):
            q2 = q_ref[pl.ds(qrow, _GRID_W), cols[hp]]
            k2 = k_ref[pl.ds(off, _BAND), cols[hp]]
            zero = jnp.zeros_like(q2)
            qq = jnp.concatenate([jnp.where(low, q2, zero), jnp.where(low, zero, q2)], axis=0)
            return lax.dot_general(qq, k2, (((1,), (1,)), ((), ())), preferred_element_type=_f32)

        def softmax(hp, s):
            bias = jnp.concatenate(
                [jnp.concatenate([t_ref[2 * hp + e, ro0 + 2 * m] for m in range(_WIN_H // 2)], axis=-1)
                 for e in range(2)], axis=0)
            s = s + bias
            p = jnp.exp(s - jnp.max(s, axis=-1, keepdims=True))
            return p.astype(_bf16), jnp.sum(p, axis=-1, keepdims=True)

        def values(hp, ps):
            v2 = v_ref[pl.ds(off, _BAND), cols[hp]]
            o = _dot(ps[0], v2) / ps[1]
            att[pl.ds(qrow, _GRID_W), cols[hp]] = jnp.where(low, o[:_GRID_W], o[_GRID_W:])

        s_q, p_q = {}, {}
        for n in range(n_pairs + 2 * _ATT_SKEW):
            if n < n_pairs:
                s_q[n] = scores(n)
            if 0 <= n - _ATT_SKEW < n_pairs:
                p_q[n - _ATT_SKEW] = softmax(n - _ATT_SKEW, s_q.pop(n - _ATT_SKEW))
            if 0 <= n - 2 * _ATT_SKEW < n_pairs:
                values(n - 2 * _ATT_SKEW, p_q.pop(n - 2 * _ATT_SKEW))
        return carry

    lax.fori_loop(0, _Q_ROWS, row_body, 0)

    half = att.shape[0] // 2
    for r in (slice(0, half), slice(half, 2 * half)):
        y = _dot(att[r, :].astype(_bf16), wo_ref[...]) + bo_ref[...]
        z = _ALPHA * x_ref[r, :] + gate_ref[...] * y
        o_ref[r, :] = _layer_norm(z, lng_ref[...], lnb_ref[...])


def _attn_layer(x, sc, sh, gate, w_qkv, b_qkv, rpb, w_o, b_o, ln_g, ln_b):
    bsz, slen, _ = x.shape
    nt = bsz * slen
    bps = slen // _TM
    x2 = x.reshape(nt, _D)
    row_spec = pl.BlockSpec((_TM, _D), lambda i: (i, 0))
    seq_spec = pl.BlockSpec((None, 1, _D), lambda i: (i // bps, 0, 0))
    qkv_shape = jax.ShapeDtypeStruct((nt, _D), _bf16)
    q, k, v = pl.pallas_call(
        _qkv_kernel,
        grid=(nt // _TM,),
        in_specs=[row_spec, seq_spec, seq_spec, _const_spec((_D, 3 * _D)), _const_spec((1, 3 * _D))],
        out_specs=[row_spec] * 3,
        out_shape=[qkv_shape] * 3,
        compiler_params=_params(("arbitrary",)),
        name="attn_qkv",
    )(x2, sc.reshape(bsz, 1, _D), sh.reshape(bsz, 1, _D), w_qkv, b_qkv.reshape(1, 3 * _D))

    grid_rows = slen // _GRID_W
    nqb = grid_rows // _Q_ROWS
    blk = _Q_ROWS * _GRID_W
    cur = pl.BlockSpec((blk, _D), lambda b, j: (b * nqb + j, 0))
    assert grid_rows >= _KV_ROWS and grid_rows % _Q_ROWS == 0
    window = pl.BlockSpec((pl.Element(_KV_ROWS * _GRID_W), pl.Element(_D)),
                          lambda b, j: (pl.multiple_of((b * grid_rows + _kv_window_start(j, grid_rows)) * _GRID_W,
                                                       _GRID_W), 0))
    bseq = pl.BlockSpec((None, 1, _D), lambda b, j: (b, 0, 0))
    table = _bias_table(rpb)
    out = pl.pallas_call(
        functools.partial(_attn_kernel, grid_rows=grid_rows),
        grid=(bsz, nqb),
        in_specs=[cur, window, window, _const_spec(table.shape), cur, bseq,
                  _const_spec((_D, _D)), _const_spec((1, _D)), _const_spec((1, _D)), _const_spec((1, _D))],
        out_specs=cur,
        out_shape=jax.ShapeDtypeStruct((nt, _D), _f32),
        scratch_shapes=[pltpu.VMEM((blk, _D), _f32)],
        compiler_params=_params(("arbitrary", "arbitrary")),
        name="nbr_attn_ln",
    )(q, k, v, table, x2, gate.reshape(bsz, 1, _D), w_o, b_o.reshape(1, _D),
      ln_g.reshape(1, _D), ln_b.reshape(1, _D))
    return out.reshape(bsz, slen, _D)


def _trunk(x, mod, s5_tables, s5_d, w_glu, b_glu, w_qkv, b_qkv, rpb, w_o, b_o,
           ffn_w_in, ffn_b_in, ffn_conv_w, ffn_conv_b, ffn_w_out, ffn_b_out, ln_g, ln_b):
    for i in range(_DEPTH):
        sh_m, sc_m, g_m, sh_f, sc_f, g_f = [mod[i, :, n * _D:(n + 1) * _D] for n in range(6)]
        if i % 2 == 0:
            x = _s5_layer(x, sc_m, sh_m, g_m, s5_tables, s5_d, w_glu, b_glu, ln_g[i, 0], ln_b[i, 0])
        else:
            x = _attn_layer(x, sc_m, sh_m, g_m, w_qkv, b_qkv, rpb, w_o, b_o, ln_g[i, 0], ln_b[i, 0])
        x = _ffn_layer(x, sc_f, sh_f, g_f, i, ffn_w_in, ffn_b_in, ffn_conv_w, ffn_conv_b,
                       ffn_w_out, ffn_b_out, ln_g[i, 1], ln_b[i, 1])
    return x


def kernel(x_prompt, x_sample, c_prompt, c_sample, w_ada, b_ada, ln_g, ln_b, s5_lam_re, s5_lam_im, s5_log_dt, s5_b_re, s5_b_im, s5_c_re, s5_c_im, s5_d, s5_w_glu, s5_b_glu, na_w_qkv, na_b_qkv, na_rpb, na_w_o, na_b_o, ffn_w_in, ffn_b_in, ffn_conv_w, ffn_conv_b, ffn_w_out, ffn_b_out):
    assert _DEPTH == 2 and x_prompt.shape[-1] == _D
    nb_p, nb_s = c_prompt.shape[0], c_sample.shape[0]
    nb_pad = -(-(nb_p + nb_s) // _SUBLANES) * _SUBLANES
    c_all = jnp.zeros((nb_pad, _D), _f32).at[:nb_p].set(c_prompt).at[nb_p:nb_p + nb_s].set(c_sample)
    mod = _modulation(c_all, w_ada, b_ada)

    tables = _s5_slab_tables(s5_lam_re[0], s5_lam_im[0], s5_log_dt[0], s5_b_re[0], s5_b_im[0],
                        s5_c_re[0], s5_c_im[0])
    shared = (tables, s5_d[0], s5_w_glu[0].astype(_bf16), s5_b_glu[0],
              na_w_qkv[0].astype(_bf16), na_b_qkv[0], na_rpb[0], na_w_o[0].astype(_bf16), na_b_o[0],
              ffn_w_in.astype(_bf16), ffn_b_in, ffn_conv_w, ffn_conv_b, ffn_w_out.astype(_bf16), ffn_b_out,
              ln_g, ln_b)
    y_prompt = _trunk(x_prompt, mod[:, :nb_p], *shared)
    y_sample = _trunk(x_sample, mod[:, nb_p:nb_p + nb_s], *shared)
    return (y_prompt, y_sample)
```

```python
import functools

import numpy as np
import jax
import jax.numpy as jnp
from jax import lax
from jax.experimental import pallas as pl
from jax.experimental.pallas import tpu as pltpu

_D = 1024
_DEPTH = 2
_GRID_W = 64
_GROUP_DIM = 16
_N_GROUPS = _D // _GROUP_DIM
_STATE_DIM = 64
_N_HEADS = 16
_HEAD_DIM = _D // _N_HEADS
_WIN_H = 8
_WIN_W = 16
_D_FF = 2816
_ALPHA = (2 * _DEPTH) ** 0.25
_LN_EPS = 1e-5

_LANES = 128
_SUBLANES = 8
_BF16_ROWS = 16
_VMEM_LIMIT = 56 * 1024 * 1024

_L = 16
_GW = _L * _GROUP_DIM
_SG = 4
_SLAB_LANES = _SG * _GROUP_DIM
_N_SLABS = _N_GROUPS // _SG
_N_TILES = _D // _LANES
_CW = _L * _SLAB_LANES
_CS = _SG * _STATE_DIM
_TS = 2 * _CS
_STATE_W = _N_GROUPS * _STATE_DIM
_S5_ROWS = 256
_S5_PARTS = 1
_SCAN_ROWS = 256

_TM = 1024
_GLU_PARTS = 2
_FFN_TM = 1024
_FFN_SPLIT = (0, 1024, 2048, _D_FF)
_Q_ROWS = 16
_KV_ROWS = _Q_ROWS + _WIN_H
_BAND = _WIN_H * _GRID_W
_ATT_SKEW = 3
_NEG = -1e30

_f32 = jnp.float32
_bf16 = jnp.bfloat16


def _params(sem):
    return pltpu.CompilerParams(dimension_semantics=sem, vmem_limit_bytes=_VMEM_LIMIT)


def _const_spec(shape):
    nd = len(shape)
    return pl.BlockSpec(shape, lambda *_: (0,) * nd, pipeline_mode=pl.Buffered(1))


def _layer_norm(z, g, b):
    mu = jnp.mean(z, axis=-1, keepdims=True)
    d = z - mu
    var = jnp.mean(d * d, axis=-1, keepdims=True)
    return d * lax.rsqrt(var + _LN_EPS) * g + b


def _dot(a, b):
    return jnp.dot(a, b, preferred_element_type=_f32)


def _mod_kernel(c_ref, w_ref, b_ref, o_ref):
    cond = jax.nn.silu(c_ref[...]).astype(_bf16)
    o_ref[...] = _dot(cond, w_ref[...].astype(_bf16)) + b_ref[...]


def _modulation(c_all, w_ada, b_ada):
    nb = c_all.shape[0]
    tn = 1536
    return pl.pallas_call(
        _mod_kernel,
        grid=(_DEPTH, 6 * _D // tn),
        in_specs=[
            pl.BlockSpec((nb, _D), lambda l, n: (0, 0)),
            pl.BlockSpec((None, _D, tn), lambda l, n: (l, 0, n)),
            pl.BlockSpec((None, 1, tn), lambda l, n: (l, 0, n)),
        ],
        out_specs=pl.BlockSpec((None, nb, tn), lambda l, n: (l, 0, n)),
        out_shape=jax.ShapeDtypeStruct((_DEPTH, nb, 6 * _D), _f32),
        compiler_params=_params(("arbitrary", "arbitrary")),
        name="adaln_mod",
    )(c_all, w_ada, b_ada.reshape(_DEPTH, 1, 6 * _D))


def _s5_tables(lam_re, lam_im, log_dt, b_re, b_im, c_re, c_im):
    hi = lax.Precision.HIGHEST
    dt = jnp.exp(log_dt)[..., None]
    z_re, z_im = lam_re * dt, lam_im * dt
    mag = jnp.exp(z_re)
    ab_re, ab_im = mag * jnp.cos(z_im), mag * jnp.sin(z_im)
    den = lam_re * lam_re + lam_im * lam_im
    nr, ni = ab_re - 1.0, ab_im
    f_re = (nr * lam_re + ni * lam_im) / den
    f_im = (ni * lam_re - nr * lam_im) / den
    bb_re = f_re[..., None] * b_re - f_im[..., None] * b_im
    bb_im = f_re[..., None] * b_im + f_im[..., None] * b_re
    bb_re = bb_re.transpose(0, 3, 1, 2).reshape(2, _GROUP_DIM, _STATE_W)
    bb_im = bb_im.transpose(0, 3, 1, 2).reshape(2, _GROUP_DIM, _STATE_W)

    k = jnp.arange(_L + 1, dtype=_f32)[:, None, None]
    zf_re, zf_im = z_re.reshape(2, _STATE_W), z_im.reshape(2, _STATE_W)
    pw_mag = jnp.exp(k * zf_re)
    pw_re, pw_im = pw_mag * jnp.cos(k * zf_im), pw_mag * jnp.sin(k * zf_im)

    abb_re = pw_re[:, :, None, :] * bb_re - pw_im[:, :, None, :] * bb_im
    abb_im = pw_re[:, :, None, :] * bb_im + pw_im[:, :, None, :] * bb_re

    by_group = (_L, 2, _GROUP_DIM, _N_GROUPS, _STATE_DIM)
    lag = (jnp.einsum('kdhgp,dgop->kdgho', abb_re[:_L].reshape(by_group), c_re, precision=hi)
           - jnp.einsum('kdhgp,dgop->kdgho', abb_im[:_L].reshape(by_group), c_im, precision=hi))
    lag_f, lag_b = lag[:, 0], lag[:, 1]
    both = jnp.concatenate([lag_b[:0:-1], (lag_f[0] + lag_b[0])[None], lag_f[1:]], axis=0)
    both = both.reshape(2 * _L - 1, _N_SLABS, _SG, _GROUP_DIM, _GROUP_DIM).transpose(1, 0, 3, 2, 4)
    both = both.reshape(_N_SLABS, 2 * _L - 1, _GROUP_DIM, _SLAB_LANES)
    toep = jnp.concatenate([both[:, :-1], both[:, 1:]], axis=-1)

    rev = np.arange(_L - 1, -1, -1)
    st = jnp.stack([abb_re[rev, 0], abb_im[rev, 0], abb_re[:_L, 1], abb_im[:_L, 1]], axis=0)
    w_state = st.reshape(4, _GW, _STATE_W)

    ct_re = c_re.transpose(0, 2, 1, 3).reshape(2, _GROUP_DIM, _STATE_W)
    ct_im = c_im.transpose(0, 2, 1, 3).reshape(2, _GROUP_DIM, _STATE_W)

    def out_mat(d, powers):
        pr_, pi_ = pw_re[powers, d][:, None, :], pw_im[powers, d][:, None, :]
        wr = ct_re[d] * pr_ - ct_im[d] * pi_
        wi = ct_re[d] * pi_ + ct_im[d] * pr_
        return wr, -wi

    wfr, wfi = out_mat(0, np.arange(1, _L + 1))
    wbr, wbi = out_mat(1, np.arange(_L, 0, -1))
    w_out = jnp.stack([wfr, wfi, wbr, wbi], axis=0).reshape(4, _GW, _STATE_W)

    flow = np.stack([np.arange(_SUBLANES), np.arange(_SUBLANES - 1, -1, -1)])[:, :, None, None]

    def chunk_pow(m):
        e = jnp.asarray(m, _f32) * float(_L)
        pm = jnp.exp(e * z_re[:, None])
        return jnp.stack([pm * jnp.cos(e * z_im[:, None]), pm * jnp.sin(e * z_im[:, None])], axis=1)

    def masked(d):
        return chunk_pow(np.full_like(flow, d)) * jnp.asarray(flow >= d, _f32)[:, None]

    scan_tab = jnp.stack([chunk_pow(np.ones_like(flow)), masked(1), masked(2), masked(4), chunk_pow(flow)], axis=1)
    scan_tab = scan_tab.reshape(2, 5, 2, _SUBLANES, _STATE_W)
    return toep, w_state, w_out, scan_tab


def _expand_toep_kernel(z_ref, o_ref):
    shape = (_SLAB_LANES, _LANES)
    row_group = lax.shift_right_logical(lax.broadcasted_iota(jnp.int32, shape, 0), _GROUP_DIM.bit_length() - 1)
    lane_group = lax.shift_right_logical(lax.broadcasted_iota(jnp.int32, shape, 1) & (_SLAB_LANES - 1),
                                         _GROUP_DIM.bit_length() - 1)
    keep = row_group == lane_group
    for s in range(_L):
        for m in range(_L // 2):
            strip = z_ref[2 * m - s + _L - 1]
            blk = jnp.concatenate([strip] * _SG, axis=0)
            o_ref[s * _SLAB_LANES:(s + 1) * _SLAB_LANES, m * _LANES:(m + 1) * _LANES] = (
                jnp.where(keep, blk, 0.0).astype(_bf16))


def _expand_toep(strips):
    return pl.pallas_call(
        _expand_toep_kernel,
        grid=(_N_SLABS,),
        in_specs=[pl.BlockSpec((None,) + strips.shape[1:], lambda q: (q, 0, 0, 0))],
        out_specs=pl.BlockSpec((None, _CW, _CW), lambda q: (q, 0, 0)),
        out_shape=jax.ShapeDtypeStruct((_N_SLABS, _CW, _CW), _bf16),
        compiler_params=_params(("arbitrary",)),
        name="s5_expand_toep",
    )(strips)


def _group_spread(n, run, g):
    j = lax.broadcasted_iota(jnp.int32, (n, _SG * n), 0)
    c = lax.broadcasted_iota(jnp.int32, (n, _SG * n), 1)
    shift = run.bit_length() - 1
    spread = lax.shift_left(lax.shift_right_logical(j, shift), shift + _SG.bit_length() - 1) + (j & (run - 1))
    return jnp.where(c == spread + g * run, 1.0, 0.0).astype(_bf16)


def _expand_state_kernel(a_ref, o_ref):
    lane_group = lax.shift_right_logical(
        lax.broadcasted_iota(jnp.int32, (_GROUP_DIM, _CS), 1), _STATE_DIM.bit_length() - 1)
    for c in range(4):
        for g in range(_SG):
            for s in range(_L):
                blk = a_ref[c, s * _GROUP_DIM:(s + 1) * _GROUP_DIM, :]
                r0 = (s * _SG + g) * _GROUP_DIM
                o_ref[r0:r0 + _GROUP_DIM, c * _CS:(c + 1) * _CS] = jnp.where(lane_group == g, blk, 0.0).astype(_bf16)


def _expand_out_kernel(a_ref, o_ref):
    for c in range(4):
        xt = a_ref[c].T.astype(_bf16)
        for g in range(_SG):
            rows = slice(g * _STATE_DIM, (g + 1) * _STATE_DIM)
            placed = _dot(xt[rows, :], _group_spread(_GW, _GROUP_DIM, g))
            o_ref[c * _CS + g * _STATE_DIM:c * _CS + (g + 1) * _STATE_DIM, :] = placed.astype(_bf16)


def _expand_states(kern, table):
    return pl.pallas_call(
        kern,
        grid=(_N_SLABS,),
        in_specs=[pl.BlockSpec((4, _GW, _CS), lambda q: (0, 0, q))],
        out_specs=pl.BlockSpec((None, _CW, _CW), lambda q: (q, 0, 0)),
        out_shape=jax.ShapeDtypeStruct((_N_SLABS, _CW, _CW), _bf16),
        compiler_params=_params(("arbitrary",)),
        name="s5_expand_states",
    )(table)


def _s5_slab_tables(*s5_params):
    toep, w_state, w_out, scan_tab = _s5_tables(*s5_params)
    return (_expand_toep(toep), _expand_states(_expand_state_kernel, w_state),
            _expand_states(_expand_out_kernel, w_out), scan_tab)


def _chunk_tokens(x_ref, sc_ref, sh_ref, c0, rows):
    sc1 = 1.0 + sc_ref[...]
    sh = sh_ref[...]
    return [x_ref[pl.ds(c0 * _L + s, rows, stride=_L), :] * sc1 + sh for s in range(_L)]


def _low_half(rows):
    return lax.broadcasted_iota(jnp.int32, (rows, _LANES), 1) < _SLAB_LANES


def _slab_chunks(us):
    low = _low_half(us[0].shape[0])
    lo, hi = [], []
    for m in range(_L // 2):
        a, b = us[2 * m], us[2 * m + 1]
        lo.append(jnp.where(low, a, pltpu.roll(b, _SLAB_LANES, axis=1)))
        hi.append(jnp.where(low, pltpu.roll(a, _SLAB_LANES, axis=1), b))
    return (jnp.concatenate(lo, axis=-1).astype(_bf16), jnp.concatenate(hi, axis=-1).astype(_bf16))


def _row_parts(x_ref):
    rows = x_ref.shape[0] // _L
    part = rows // _S5_PARTS
    return [(p * part, part) for p in range(_S5_PARTS)]


def _s5_state_kernel(x_ref, sc_ref, sh_ref, w_ref, *o_refs):
    for c0, rows in _row_parts(x_ref):
        us = _chunk_tokens(x_ref, sc_ref, sh_ref, c0, rows)
        for h, u in enumerate(_slab_chunks(us)):
            r = _dot(u, w_ref[h])
            for c, o_ref in enumerate(o_refs):
                o_ref[c0:c0 + rows, h * _CS:(h + 1) * _CS] = r[:, c * _CS:(c + 1) * _CS]


def _s5_scan_kernel(lfr, lfi, lbr, lbi, tab_ref, ofr, ofi, obr, obi, carry):
    @pl.when(pl.program_id(2) == 0)
    def _():
        carry[...] = jnp.zeros_like(carry)

    sub = lax.broadcasted_iota(jnp.int32, (_SUBLANES, lfr.shape[1]), 0)

    def cmul(ar, ai, br, bi):
        return ar * br - ai * bi, ar * bi + ai * br

    def tile_scan(xr, xi, f0r, f0i, d):
        def shift(v, k):
            return pltpu.roll(v, k if d == 0 else _SUBLANES - k, axis=0)

        ir, ii = xr, xi
        for n, k in ((1, 1), (2, 2), (3, 4)):
            pr, pi = cmul(tab_ref[d, n, 0], tab_ref[d, n, 1], shift(ir, k), shift(ii, k))
            ir, ii = ir + pr, ii + pi
        first = sub == (0 if d == 0 else _SUBLANES - 1)
        cr, ci = cmul(tab_ref[d, 4, 0], tab_ref[d, 4, 1], f0r, f0i)
        fin_r = cr + jnp.where(first, 0.0, shift(ir, 1))
        fin_i = ci + jnp.where(first, 0.0, shift(ii, 1))
        nr, ni = cmul(tab_ref[d, 0, 0], tab_ref[d, 0, 1], fin_r, fin_i)
        nr, ni = nr + xr, ni + xi
        end = _SUBLANES - 1 if d == 0 else 0
        return (fin_r, fin_i, jnp.broadcast_to(nr[end:end + 1], nr.shape),
                jnp.broadcast_to(ni[end:end + 1], ni.shape))

    def scan_two(re_ref, im_ref, o_re, o_im, row, c_re, c_im, d):
        order = (0, 1) if d == 0 else (1, 0)
        outs = [None, None]
        for n in order:
            rows = pl.ds(pl.multiple_of(row + n * _SUBLANES, _SUBLANES), _SUBLANES)
            fr, fi, c_re, c_im = tile_scan(re_ref[rows, :], im_ref[rows, :], c_re, c_im, d)
            outs[n] = (fr, fi)
        both = pl.ds(row, _BF16_ROWS)
        o_re[both, :] = jnp.concatenate([outs[0][0], outs[1][0]], axis=0).astype(_bf16)
        o_im[both, :] = jnp.concatenate([outs[0][1], outs[1][1]], axis=0).astype(_bf16)
        return c_re, c_im

    def body(i, c):
        rf = pl.multiple_of(i * _BF16_ROWS, _BF16_ROWS)
        rb = pl.multiple_of((pairs - 1 - i) * _BF16_ROWS, _BF16_ROWS)
        c0, c1 = scan_two(lfr, lfi, ofr, ofi, rf, c[0], c[1], 0)
        c2, c3 = scan_two(lbr, lbi, obr, obi, rb, c[2], c[3], 1)
        return (c0, c1, c2, c3)

    pairs = lfr.shape[0] // _BF16_ROWS
    c = lax.fori_loop(0, pairs, body, (carry[0], carry[1], carry[2], carry[3]))
    for q in range(4):
        carry[q] = c[q]


def _s5_out_kernel(x_ref, sc_ref, sh_ref, d_ref, sfr, sfi, sbr, sbi, t_ref, w_ref, o_ref):
    s_refs = (sfr, sfi, sbr, sbi)
    d = d_ref[...]
    for c0, rows in _row_parts(x_ref):
        us = _chunk_tokens(x_ref, sc_ref, sh_ref, c0, rows)
        ys = []
        for h, u in enumerate(_slab_chunks(us)):
            st = jnp.concatenate([s[c0:c0 + rows, h * _CS:(h + 1) * _CS] for s in s_refs], axis=-1)
            ys.append(_dot(u, t_ref[h]) + _dot(st, w_ref[h]))
        low = _low_half(rows)
        for t in range(_L):
            cols = slice((t // 2) * _LANES, (t // 2 + 1) * _LANES)
            y_lo, y_hi = ys[0][:, cols], ys[1][:, cols]
            if t % 2 == 0:
                y_t = jnp.where(low, y_lo, pltpu.roll(y_hi, _SLAB_LANES, axis=1))
            else:
                y_t = jnp.where(low, pltpu.roll(y_lo, _SLAB_LANES, axis=1), y_hi)
            o_ref[pl.ds(c0 * _L + t, rows, stride=_L), :] = jax.nn.gelu(d * us[t] + y_t)


def _glu_ln_kernel(y_ref, x_ref, w_ref, b_ref, gate_ref, lng_ref, lnb_ref, o_ref):
    part = y_ref.shape[0] // _GLU_PARTS
    for r in (slice(n * part, (n + 1) * part) for n in range(_GLU_PARTS)):
        ag = _dot(y_ref[r, :].astype(_bf16), w_ref[...]) + b_ref[...]
        mix = ag[:, :_D] * jax.nn.sigmoid(ag[:, _D:])
        z = _ALPHA * x_ref[r, :] + gate_ref[...] * mix
        o_ref[r, :] = _layer_norm(z, lng_ref[...], lnb_ref[...])


def _s5_layer(x, sc, sh, gate, tables, d_skip, w_glu, b_glu, ln_g, ln_b):
    bsz, slen, _ = x.shape
    cs = slen // _L
    nt = bsz * slen
    toep, w_state, w_out, scan_tab = tables
    rows = min(cs, _S5_ROWS)
    bpc = cs // rows
    nblk = bsz * bpc
    x2 = x.reshape(nt, _D)
    sc3, sh3 = sc.reshape(bsz, 1, _D), sh.reshape(bsz, 1, _D)

    x_spec = pl.BlockSpec((rows * _L, _LANES), lambda q, i: (i, q))
    mod_spec = pl.BlockSpec((None, 1, _LANES), lambda q, i: (i // bpc, 0, q))
    st_spec = pl.BlockSpec((None, rows, _TS), lambda q, i: (i // bpc, i % bpc, q))
    st_shape = jax.ShapeDtypeStruct((bsz, cs, _STATE_W), _f32)
    slab_spec = pl.BlockSpec((2, _CW, _CW), lambda q, i: (q, 0, 0))

    loc = pl.pallas_call(
        _s5_state_kernel,
        grid=(_N_TILES, nblk),
        in_specs=[x_spec, mod_spec, mod_spec, slab_spec],
        out_specs=[st_spec] * 4,
        out_shape=[st_shape] * 4,
        compiler_params=_params(("arbitrary", "arbitrary")),
        name="s5_chunk_states",
    )(x2, sc3, sh3, w_state)

    cblk = min(cs, _SCAN_ROWS)
    nsb = cs // cblk
    fwd_spec = pl.BlockSpec((None, cblk, _TS), lambda b, q, j: (b, j, q))
    bwd_spec = pl.BlockSpec((None, cblk, _TS), lambda b, q, j: (b, nsb - 1 - j, q))
    sin = pl.pallas_call(
        _s5_scan_kernel,
        grid=(bsz, _N_TILES, nsb),
        in_specs=[fwd_spec, fwd_spec, bwd_spec, bwd_spec,
                  pl.BlockSpec((2, 5, 2, _SUBLANES, _TS), lambda b, q, j: (0, 0, 0, 0, q))],
        out_specs=[fwd_spec, fwd_spec, bwd_spec, bwd_spec],
        out_shape=[jax.ShapeDtypeStruct(st_shape.shape, _bf16)] * 4,
        scratch_shapes=[pltpu.VMEM((4, _SUBLANES, _TS), _f32)],
        compiler_params=_params(("arbitrary", "arbitrary", "arbitrary")),
        name="s5_chunk_scan",
    )(*loc, scan_tab)

    y = pl.pallas_call(
        _s5_out_kernel,
        grid=(_N_TILES, nblk),
        in_specs=[x_spec, mod_spec, mod_spec, pl.BlockSpec((1, _LANES), lambda q, i: (0, q)),
                  st_spec, st_spec, st_spec, st_spec, slab_spec, slab_spec],
        out_specs=x_spec,
        out_shape=jax.ShapeDtypeStruct((nt, _D), _f32),
        compiler_params=_params(("arbitrary", "arbitrary")),
        name="s5_chunk_out",
    )(x2, sc3, sh3, d_skip.reshape(1, _D), *sin, toep, w_out)

    bps = slen // _TM
    row_spec = pl.BlockSpec((_TM, _D), lambda i: (i, 0))
    seq_spec = pl.BlockSpec((None, 1, _D), lambda i: (i // bps, 0, 0))
    out = pl.pallas_call(
        _glu_ln_kernel,
        grid=(nt // _TM,),
        in_specs=[row_spec, row_spec, _const_spec((_D, 2 * _D)), _const_spec((1, 2 * _D)),
                  seq_spec, _const_spec((1, _D)), _const_spec((1, _D))],
        out_specs=row_spec,
        out_shape=jax.ShapeDtypeStruct((nt, _D), _f32),
        compiler_params=_params(("arbitrary",)),
        name="s5_glu_ln",
    )(y, x2, w_glu, b_glu.reshape(1, 2 * _D), gate.reshape(bsz, 1, _D),
      ln_g.reshape(1, _D), ln_b.reshape(1, _D))
    return out.reshape(bsz, slen, _D)


def _ffn_kernel(xp_ref, x_ref, xn_ref, sc_ref, sh_ref, gate_ref, win_ref, bin_ref, cw_ref, cb_ref,
                wout_ref, bout_ref, lng_ref, lnb_ref, o_ref, *, blocks_per_seq):
    i = pl.program_id(0)
    first = (i % blocks_per_seq) == 0
    last = (i % blocks_per_seq) == blocks_per_seq - 1
    halo = _BF16_ROWS
    tm = x_ref.shape[0]
    rows = tm + 2 * halo

    sc1 = 1.0 + sc_ref[...]
    sh = sh_ref[...]
    x = x_ref[...]
    h = (x * sc1 + sh).astype(_bf16)
    h_prev = (xp_ref[...] * sc1 + sh).astype(_bf16)
    h_next = (xn_ref[...] * sc1 + sh).astype(_bf16)
    h_ext = jnp.concatenate([h_prev, h, h_next], axis=0)

    rid = lax.broadcasted_iota(jnp.int32, (rows, 1), 0)
    pad = (first & (rid == halo - 1)) | (last & (rid == halo + tm))

    splits = list(zip(_FFN_SPLIT[:-1], _FFN_SPLIT[1:]))
    acts = []
    for lo, hi in splits:
        u = _dot(h_ext, win_ref[:, lo:hi]) + bin_ref[:, lo:hi]
        u = jnp.where(pad, 0.0, u)
        g = _dot(h, win_ref[:, _D_FF + lo:_D_FF + hi]) + bin_ref[:, _D_FF + lo:_D_FF + hi]
        up = pltpu.roll(u, 1, axis=0)[halo:halo + tm]
        dn = pltpu.roll(u, rows - 1, axis=0)[halo:halo + tm]
        conv = (up * cw_ref[0:1, lo:hi] + u[halo:halo + tm] * cw_ref[1:2, lo:hi]
                + dn * cw_ref[2:3, lo:hi] + cb_ref[:, lo:hi])
        acts.append((jax.nn.gelu(conv) * g).astype(_bf16))
    half = tm // 2
    for r in (slice(0, half), slice(half, tm)):
        y = bout_ref[...] + sum(_dot(a[r, :], wout_ref[lo:hi, :]) for a, (lo, hi) in zip(acts, splits))
        z = _ALPHA * x[r, :] + gate_ref[...] * y
        o_ref[r, :] = _layer_norm(z, lng_ref[...], lnb_ref[...])


def _ffn_layer(x, sc, sh, gate, layer, w_in, b_in, conv_w, conv_b, w_out, b_out, ln_g, ln_b):
    bsz, slen, _ = x.shape

    def layer_spec(*shape):
        return pl.BlockSpec((None,) + shape, lambda i: (layer,) + (0,) * len(shape), pipeline_mode=pl.Buffered(1))

    nt = bsz * slen
    bps = slen // _FFN_TM
    hb = _FFN_TM // _BF16_ROWS
    n_halo = nt // _BF16_ROWS
    row_spec = pl.BlockSpec((_FFN_TM, _D), lambda i: (i, 0))
    prev_spec = pl.BlockSpec((_BF16_ROWS, _D), lambda i: (jnp.maximum(i * hb - 1, 0), 0))
    next_spec = pl.BlockSpec((_BF16_ROWS, _D), lambda i: (jnp.minimum((i + 1) * hb, n_halo - 1), 0))
    seq_spec = pl.BlockSpec((None, 1, _D), lambda i: (i // bps, 0, 0))
    x2 = x.reshape(nt, _D)
    out = pl.pallas_call(
        functools.partial(_ffn_kernel, blocks_per_seq=bps),
        grid=(nt // _FFN_TM,),
        in_specs=[prev_spec, row_spec, next_spec, seq_spec, seq_spec, seq_spec,
                  layer_spec(_D, 2 * _D_FF), layer_spec(1, 2 * _D_FF),
                  layer_spec(3, _D_FF), layer_spec(1, _D_FF),
                  layer_spec(_D_FF, _D), layer_spec(1, _D),
                  _const_spec((1, _D)), _const_spec((1, _D))],
        out_specs=row_spec,
        out_shape=jax.ShapeDtypeStruct((nt, _D), _f32),
        compiler_params=_params(("arbitrary",)),
        name="conv_ffn_ln",
    )(x2, x2, x2, sc.reshape(bsz, 1, _D), sh.reshape(bsz, 1, _D), gate.reshape(bsz, 1, _D),
      w_in, b_in.reshape(_DEPTH, 1, 2 * _D_FF), conv_w, conv_b.reshape(_DEPTH, 1, _D_FF),
      w_out, b_out.reshape(_DEPTH, 1, _D), ln_g.reshape(1, _D), ln_b.reshape(1, _D))
    return out.reshape(bsz, slen, _D)


def _qkv_kernel(x_ref, sc_ref, sh_ref, w_ref, b_ref, q_ref, k_ref, v_ref):
    h = (x_ref[...] * (1.0 + sc_ref[...]) + sh_ref[...]).astype(_bf16)
    qkv = _dot(h, w_ref[...]) + b_ref[...]
    q_ref[...] = (qkv[:, :_D] * (_HEAD_DIM ** -0.5)).astype(_bf16)
    k_ref[...] = qkv[:, _D:2 * _D].astype(_bf16)
    v_ref[...] = qkv[:, 2 * _D:].astype(_bf16)


def _bias_table(rpb):
    col = np.arange(_GRID_W)
    start = np.clip(col - _WIN_W // 2, 0, _GRID_W - _WIN_W)
    kc = col[None, :]
    inside = (kc >= start[:, None]) & (kc < start[:, None] + _WIN_W)
    off = kc - col[:, None] + (_WIN_W - 1)
    pick = ((off[None] == np.arange(2 * _WIN_W - 1)[:, None, None]) & inside[None]).astype(np.float32)
    t2 = (jnp.einsum('hrc,cqk->hrqk', rpb, pick, precision=lax.Precision.HIGHEST)
          + np.where(inside, 0.0, _NEG).astype(np.float32))
    return jnp.concatenate([t2[:, :-1], t2[:, 1:]], axis=-1)


def _kv_window_start(jb, grid_rows):
    return jnp.clip(_Q_ROWS * jb - _WIN_H // 2, 0, grid_rows - _KV_ROWS)


def _attn_kernel(q_ref, k_ref, v_ref, t_ref, x_ref, gate_ref,
                 wo_ref, bo_ref, lng_ref, lnb_ref, o_ref, att, *, grid_rows):
    jb = pl.program_id(1)
    w0 = _kv_window_start(jb, grid_rows)

    lane = lax.broadcasted_iota(jnp.int32, (_GRID_W, 2 * _HEAD_DIM), 1)
    low = lane < _HEAD_DIM
    n_pairs = _N_HEADS // 2

    def row_body(i, carry):
        r = _Q_ROWS * jb + i
        rs = jnp.clip(r - _WIN_H // 2, 0, grid_rows - _WIN_H)
        off = pl.multiple_of((rs - w0) * _GRID_W, _GRID_W)
        ro0 = rs - r + (_WIN_H - 1)
        qrow = pl.multiple_of(i * _GRID_W, _GRID_W)
        cols = [slice(hp * 2 * _HEAD_DIM, (hp + 1) * 2 * _HEAD_DIM) for hp in range(n_pairs)]

        def scores(hp):
            q2 = q_ref[pl.ds(qrow, _GRID_W), cols[hp]]
            k2 = k_ref[pl.ds(off, _BAND), cols[hp]]
            zero = jnp.zeros_like(q2)
            qq = jnp.concatenate([jnp.where(low, q2, zero), jnp.where(low, zero, q2)], axis=0)
            return lax.dot_general(qq, k2, (((1,), (1,)), ((), ())), preferred_element_type=_f32)

        def softmax(hp, s):
            bias = jnp.concatenate(
                [jnp.concatenate([t_ref[2 * hp + e, ro0 + 2 * m] for m in range(_WIN_H // 2)], axis=-1)
                 for e in range(2)], axis=0)
            s = s + bias
            p = jnp.exp(s - jnp.max(s, axis=-1, keepdims=True))
            return p.astype(_bf16), jnp.sum(p, axis=-1, keepdims=True)

        def values(hp, ps):
            v2 = v_ref[pl.ds(off, _BAND), cols[hp]]
            o = _dot(ps[0], v2) / ps[1]
            att[pl.ds(qrow, _GRID_W), cols[hp]] = jnp.where(low, o[:_GRID_W], o[_GRID_W:])

        s_q, p_q = {}, {}
        for n in range(n_pairs + 2 * _ATT_SKEW):
            if n < n_pairs:
                s_q[n] = scores(n)
            if 0 <= n - _ATT_SKEW < n_pairs:
                p_q[n - _ATT_SKEW] = softmax(n - _ATT_SKEW, s_q.pop(n - _ATT_SKEW))
            if 0 <= n - 2 * _ATT_SKEW < n_pairs:
                values(n - 2 * _ATT_SKEW, p_q.pop(n - 2 * _ATT_SKEW))
        return carry

    lax.fori_loop(0, _Q_ROWS, row_body, 0)

    half = att.shape[0] // 2
    for r in (slice(0, half), slice(half, 2 * half)):
        y = _dot(att[r, :].astype(_bf16), wo_ref[...]) + bo_ref[...]
        z = _ALPHA * x_ref[r, :] + gate_ref[...] * y
        o_ref[r, :] = _layer_norm(z, lng_ref[...], lnb_ref[...])


def _attn_layer(x, sc, sh, gate, w_qkv, b_qkv, rpb, w_o, b_o, ln_g, ln_b):
    bsz, slen, _ = x.shape
    nt = bsz * slen
    bps = slen // _TM
    x2 = x.reshape(nt, _D)
    row_spec = pl.BlockSpec((_TM, _D), lambda i: (i, 0))
    seq_spec = pl.BlockSpec((None, 1, _D), lambda i: (i // bps, 0, 0))
    qkv_shape = jax.ShapeDtypeStruct((nt, _D), _bf16)
    q, k, v = pl.pallas_call(
        _qkv_kernel,
        grid=(nt // _TM,),
        in_specs=[row_spec, seq_spec, seq_spec, _const_spec((_D, 3 * _D)), _const_spec((1, 3 * _D))],
        out_specs=[row_spec] * 3,
        out_shape=[qkv_shape] * 3,
        compiler_params=_params(("arbitrary",)),
        name="attn_qkv",
    )(x2, sc.reshape(bsz, 1, _D), sh.reshape(bsz, 1, _D), w_qkv, b_qkv.reshape(1, 3 * _D))

    grid_rows = slen // _GRID_W
    nqb = grid_rows // _Q_ROWS
    blk = _Q_ROWS * _GRID_W
    cur = pl.BlockSpec((blk, _D), lambda b, j: (b * nqb + j, 0))
    assert grid_rows >= _KV_ROWS and grid_rows % _Q_ROWS == 0
    window = pl.BlockSpec((pl.Element(_KV_ROWS * _GRID_W), pl.Element(_D)),
                          lambda b, j: (pl.multiple_of((b * grid_rows + _kv_window_start(j, grid_rows)) * _GRID_W,
                                                       _GRID_W), 0))
    bseq = pl.BlockSpec((None, 1, _D), lambda b, j: (b, 0, 0))
    table = _bias_table(rpb)
    out = pl.pallas_call(
        functools.partial(_attn_kernel, grid_rows=grid_rows),
        grid=(bsz, nqb),
        in_specs=[cur, window, window, _const_spec(table.shape), cur, bseq,
                  _const_spec((_D, _D)), _const_spec((1, _D)), _const_spec((1, _D)), _const_spec((1, _D))],
        out_specs=cur,
        out_shape=jax.ShapeDtypeStruct((nt, _D), _f32),
        scratch_shapes=[pltpu.VMEM((blk, _D), _f32)],
        compiler_params=_params(("arbitrary", "arbitrary")),
        name="nbr_attn_ln",
    )(q, k, v, table, x2, gate.reshape(bsz, 1, _D), w_o, b_o.reshape(1, _D),
      ln_g.reshape(1, _D), ln_b.reshape(1, _D))
    return out.reshape(bsz, slen, _D)


def _trunk(x, mod, s5_tables, s5_d, w_glu, b_glu, w_qkv, b_qkv, rpb, w_o, b_o,
           ffn_w_in, ffn_b_in, ffn_conv_w, ffn_conv_b, ffn_w_out, ffn_b_out, ln_g, ln_b):
    for i in range(_DEPTH):
        sh_m, sc_m, g_m, sh_f, sc_f, g_f = [mod[i, :, n * _D:(n + 1) * _D] for n in range(6)]
        if i % 2 == 0:
            x = _s5_layer(x, sc_m, sh_m, g_m, s5_tables, s5_d, w_glu, b_glu, ln_g[i, 0], ln_b[i, 0])
        else:
            x = _attn_layer(x, sc_m, sh_m, g_m, w_qkv, b_qkv, rpb, w_o, b_o, ln_g[i, 0], ln_b[i, 0])
        x = _ffn_layer(x, sc_f, sh_f, g_f, i, ffn_w_in, ffn_b_in, ffn_conv_w, ffn_conv_b,
                       ffn_w_out, ffn_b_out, ln_g[i, 1], ln_b[i, 1])
    return x


def kernel(x_prompt, x_sample, c_prompt, c_sample, w_ada, b_ada, ln_g, ln_b, s5_lam_re, s5_lam_im, s5_log_dt, s5_b_re, s5_b_im, s5_c_re, s5_c_im, s5_d, s5_w_glu, s5_b_glu, na_w_qkv, na_b_qkv, na_rpb, na_w_o, na_b_o, ffn_w_in, ffn_b_in, ffn_conv_w, ffn_conv_b, ffn_w_out, ffn_b_out):
    assert _DEPTH == 2 and x_prompt.shape[-1] == _D
    nb_p, nb_s = c_prompt.shape[0], c_sample.shape[0]
    nb_pad = -(-(nb_p + nb_s) // _SUBLANES) * _SUBLANES
    c_all = jnp.zeros((nb_pad, _D), _f32).at[:nb_p].set(c_prompt).at[nb_p:nb_p + nb_s].set(c_sample)
    mod = _modulation(c_all, w_ada, b_ada)

    tables = _s5_slab_tables(s5_lam_re[0], s5_lam_im[0], s5_log_dt[0], s5_b_re[0], s5_b_im[0],
                        s5_c_re[0], s5_c_im[0])
    shared = (tables, s5_d[0], s5_w_glu[0].astype(_bf16), s5_b_glu[0],
              na_w_qkv[0].astype(_bf16), na_b_qkv[0], na_rpb[0], na_w_o[0].astype(_bf16), na_b_o[0],
              ffn_w_in.astype(_bf16), ffn_b_in, ffn_conv_w, ffn_conv_b, ffn_w_out.astype(_bf16), ffn_b_out,
              ln_g, ln_b)
    y_prompt = _trunk(x_prompt, mod[:, :nb_p], *shared)
    y_sample = _trunk(x_sample, mod[:, nb_p:nb_p + nb_s], *shared)
    return (y_prompt, y_sample)
```
